```python
import jax, jax.numpy as jnp
from jax import lax
import numpy as np

D_MODEL = 2048
BATCH = 8
SEQ = 2048
DEPTH = 2

MIX_WIDTH = D_MODEL
N_GROUPS = 4
GROUP_WIDTH = MIX_WIDTH // N_GROUPS
POOL_WINDOWS = (2, 4, 8, 16)
POOL_GROUP = GROUP_WIDTH // len(POOL_WINDOWS)
SGU_HEADS = 4
SGU_HEAD_DIM = GROUP_WIDTH // SGU_HEADS
SGU_CHUNK = 128
FOX_HEADS = 4
FOX_HEAD_DIM = GROUP_WIDTH // FOX_HEADS
FOX_BLOCK = 128
RET_HEADS = 4
RET_V_DIM = GROUP_WIDTH // RET_HEADS
RET_QK_DIM = RET_V_DIM // 2
RET_CHUNK = 128
ROPE_BASE = 10000.0
D_FF = 4 * D_MODEL
EPS = 1e-6
NEG_BIG = -1e30
SPLIT_SIZES = (GROUP_WIDTH,
               GROUP_WIDTH, GROUP_WIDTH,
               GROUP_WIDTH, GROUP_WIDTH, GROUP_WIDTH, FOX_HEADS,
               RET_HEADS * RET_QK_DIM, RET_HEADS * RET_QK_DIM, GROUP_WIDTH, GROUP_WIDTH)
IN_COLS = 9 * GROUP_WIDTH + FOX_HEADS + 2 * RET_HEADS * RET_QK_DIM

kernel_name = 'hymba_style_pool_sgu_fox_retention_hybrid'


def split_columns(proj):
    pieces, start = [], 0
    for size in SPLIT_SIZES:
        pieces.append(proj[..., start:start + size])
        start += size
    return pieces


def rmsnorm(x, g):
    xf = x.astype(jnp.float32)
    y = xf * lax.rsqrt(jnp.mean(xf * xf, axis=-1, keepdims=True) + EPS)
    return (y * g.astype(jnp.float32)).astype(x.dtype)


def head_layernorm(x, g, n_heads, head_dim):
    B, S, _ = x.shape
    xf = x.astype(jnp.float32).reshape(B, S, n_heads, head_dim)
    mu = jnp.mean(xf, axis=-1, keepdims=True)
    var = jnp.mean(jnp.square(xf - mu), axis=-1, keepdims=True)
    y = (xf - mu) * lax.rsqrt(var + EPS) * g.astype(jnp.float32).reshape(n_heads, head_dim)
    return y.reshape(B, S, n_heads * head_dim).astype(x.dtype)


def pool_mixer(a, pool_w, pool_scale):
    B, S, _ = a.shape
    af = a.astype(jnp.float32)
    cs = jnp.cumsum(af, axis=1)
    outs = []
    for gi, w in enumerate(POOL_WINDOWS):
        lo, hi = gi * POOL_GROUP, (gi + 1) * POOL_GROUP
        c = cs[..., lo:hi]
        c_prev = jnp.pad(c[:, :S - w], ((0, 0), (w, 0), (0, 0)))
        cnt = jnp.minimum(jnp.arange(1, S + 1, dtype=jnp.float32), float(w))[None, :, None]
        outs.append((c - c_prev) / cnt - af[..., lo:hi])
    p = jnp.stack(outs, axis=2).astype(a.dtype)
    y = jnp.einsum('bsgc,gcd->bsgd', p, pool_w).reshape(B, S, GROUP_WIDTH)
    return y * pool_scale


def sgu_mixer(u, v, norm_g, w_s, b_s):
    B, S, _ = u.shape
    u = jax.nn.gelu(u)
    v = head_layernorm(jax.nn.gelu(v), norm_g, SGU_HEADS, SGU_HEAD_DIM)
    vc = v.reshape(B, S // SGU_CHUNK, SGU_CHUNK, SGU_HEADS, SGU_HEAD_DIM)
    mask = jnp.tril(jnp.ones((SGU_CHUNK, SGU_CHUNK), dtype=jnp.float32)).astype(w_s.dtype)
    z = jnp.einsum('hts,bcshd->bcthd', w_s * mask[None], vc)
    z = z + jnp.transpose(b_s)[None, None, :, :, None]
    return u * z.reshape(B, S, GROUP_WIDTH)


def fox_mixer(q, k, v, f_logit, b_f):
    B, S, _ = q.shape
    q = q.reshape(B, S, FOX_HEADS, FOX_HEAD_DIM).transpose(0, 2, 1, 3)
    k = k.reshape(B, S, FOX_HEADS, FOX_HEAD_DIM).transpose(0, 2, 1, 3)
    v = v.reshape(B, S, FOX_HEADS, FOX_HEAD_DIM).transpose(0, 2, 1, 3)
    log_f = jax.nn.log_sigmoid(f_logit.astype(jnp.float32) + b_f.astype(jnp.float32))
    c = jnp.cumsum(log_f, axis=1).transpose(0, 2, 1)
    scale = FOX_HEAD_DIM ** -0.5
    outs = []
    for i in range(S // FOX_BLOCK):
        start, end = i * FOX_BLOCK, (i + 1) * FOX_BLOCK
        qb = q[:, :, start:end]
        kb = k[:, :, :end]
        vb = v[:, :, :end]
        cq = c[:, :, start:end]
        ck = c[:, :, :end]
        s = jnp.einsum('bhqd,bhkd->bhqk', qb, kb).astype(jnp.float32) * scale
        s = s + (cq[..., :, None] - ck[..., None, :])
        qpos = start + jnp.arange(FOX_BLOCK)
        kpos = jnp.arange(end)
        s = jnp.where(kpos[None, :] <= qpos[:, None], s, NEG_BIG)
        p = jax.nn.softmax(s, axis=-1)
        outs.append(jnp.einsum('bhqk,bhkd->bhqd', p.astype(vb.dtype), vb))
    o = jnp.concatenate(outs, axis=2)
    return o.transpose(0, 2, 1, 3).reshape(B, S, GROUP_WIDTH)


def rotary(x, pos):
    half = x.shape[-1] // 2
    inv = jnp.exp(-(jnp.arange(half, dtype=jnp.float32) / half) * np.float32(np.log(ROPE_BASE)))
    ang = pos[:, None] * inv[None, :]
    cos = jnp.cos(ang)[None, :, None, :]
    sin = jnp.sin(ang)[None, :, None, :]
    x1, x2 = x[..., :half], x[..., half:]
    return jnp.concatenate([x1 * cos - x2 * sin, x1 * sin + x2 * cos], axis=-1)


def retention_mixer(q, k, v, g, norm_g):
    B, S, _ = q.shape
    L = RET_CHUNK
    nC = S // L
    pos = jnp.arange(S, dtype=jnp.float32)
    qf = rotary(q.astype(jnp.float32).reshape(B, S, RET_HEADS, RET_QK_DIM), pos)
    kf = rotary(k.astype(jnp.float32).reshape(B, S, RET_HEADS, RET_QK_DIM), pos) * (RET_QK_DIM ** -0.5)
    vf = v.astype(jnp.float32).reshape(B, S, RET_HEADS, RET_V_DIM)
    gamma = 1.0 - jnp.exp((-5.0 - jnp.arange(RET_HEADS, dtype=jnp.float32)) * np.float32(np.log(2.0)))
    log_gamma = jnp.log(gamma)
    l = jnp.arange(L, dtype=jnp.float32)
    diff = l[:, None] - l[None, :]
    decay = jnp.where(diff[None] >= 0, jnp.exp(jnp.maximum(diff, 0.0)[None] * log_gamma[:, None, None]), 0.0)
    qc = qf.reshape(B, nC, L, RET_HEADS, RET_QK_DIM)
    kc = kf.reshape(B, nC, L, RET_HEADS, RET_QK_DIM)
    vc = vf.reshape(B, nC, L, RET_HEADS, RET_V_DIM)
    s = jnp.einsum('bclhd,bcmhd->bchlm', qc, kc) * decay[None, None]
    y_intra = jnp.einsum('bchlm,bcmhe->bclhe', s, vc)
    xi = jnp.exp((l + 1.0)[:, None] * log_gamma[None, :])
    zeta = jnp.exp((L - 1.0 - l)[:, None] * log_gamma[None, :])
    chunk_decay = jnp.exp(L * log_gamma)

    def step(R, inp):
        qi, ki, vi = inp
        y = jnp.einsum('blhd,bhde->blhe', qi, R) * xi[None, :, :, None]
        R = R * chunk_decay[None, :, None, None] + jnp.einsum('blhd,blhe->bhde', ki * zeta[None, :, :, None], vi)
        return R, y

    R0 = jnp.zeros((B, RET_HEADS, RET_QK_DIM, RET_V_DIM), jnp.float32)
    xs = (jnp.moveaxis(qc, 1, 0), jnp.moveaxis(kc, 1, 0), jnp.moveaxis(vc, 1, 0))
    _, y_cross = lax.scan(step, R0, xs)
    y = (y_intra + jnp.moveaxis(y_cross, 0, 1)).reshape(B, S, GROUP_WIDTH)
    y = head_layernorm(y, norm_g, RET_HEADS, RET_V_DIM).astype(g.dtype)
    return jax.nn.silu(g) * y


def setup_inputs(seed: int = 0) -> dict:
    key = jax.random.key(seed)
    ks = jax.random.split(key, 16)
    f32 = jnp.float32

    def nrm(k, shape, s):
        return jax.random.normal(k, shape, f32) * s

    return {
        'x': jax.random.normal(ks[0], (BATCH, SEQ, D_MODEL), f32),
        'norm_mix_g': 1.0 + nrm(ks[1], (DEPTH, D_MODEL), 0.05),
        'w_in': nrm(ks[2], (DEPTH, D_MODEL, IN_COLS), D_MODEL ** -0.5),
        'fox_b_f': 2.0 + nrm(ks[3], (DEPTH, FOX_HEADS), 0.5),
        'pool_w': nrm(ks[4], (DEPTH, len(POOL_WINDOWS), POOL_GROUP, POOL_GROUP), POOL_GROUP ** -0.5),
        'pool_scale': 1.0 + nrm(ks[5], (DEPTH, GROUP_WIDTH), 0.1),
        'sgu_norm_g': 1.0 + nrm(ks[6], (DEPTH, GROUP_WIDTH), 0.05),
        'sgu_w_s': nrm(ks[7], (DEPTH, SGU_HEADS, SGU_CHUNK, SGU_CHUNK), SGU_CHUNK ** -0.5),
        'sgu_b': 1.0 + nrm(ks[8], (DEPTH, SGU_HEADS, SGU_CHUNK), 0.1),
        'ret_norm_g': 1.0 + nrm(ks[9], (DEPTH, GROUP_WIDTH), 0.05),
        'w_out': nrm(ks[10], (DEPTH, MIX_WIDTH, D_MODEL), MIX_WIDTH ** -0.5),
        'norm_mlp_g': 1.0 + nrm(ks[11], (DEPTH, D_MODEL), 0.05),
        'w_ff1': nrm(ks[12], (DEPTH, D_MODEL, D_FF), D_MODEL ** -0.5),
        'w_ff2': nrm(ks[13], (DEPTH, D_FF, D_MODEL), D_FF ** -0.5),
        'norm_final_g': 1.0 + nrm(ks[14], (D_MODEL,), 0.05),
    }


def reference(x, norm_mix_g, w_in, fox_b_f, pool_w, pool_scale, sgu_norm_g, sgu_w_s, sgu_b,
              ret_norm_g, w_out, norm_mlp_g, w_ff1, w_ff2, norm_final_g):
    h = x
    for layer in range(DEPTH):
        xn = rmsnorm(h, norm_mix_g[layer])
        proj = jnp.einsum('bsd,dc->bsc', xn, w_in[layer])
        (a_in, u_in, v_in, q_f, k_f, v_f, f_logit,
         q_r, k_r, v_r, g_r) = split_columns(proj)
        y_a = pool_mixer(a_in, pool_w[layer], pool_scale[layer])
        y_b = sgu_mixer(u_in, v_in, sgu_norm_g[layer], sgu_w_s[layer], sgu_b[layer])
        y_c = fox_mixer(q_f, k_f, v_f, f_logit, fox_b_f[layer])
        y_d = retention_mixer(q_r, k_r, v_r, g_r, ret_norm_g[layer])
        mix = jnp.concatenate([y_a, y_b, y_c, y_d], axis=-1)
        h = h + jnp.einsum('bsc,cd->bsd', mix, w_out[layer])
        hn = rmsnorm(h, norm_mlp_g[layer])
        ff = jnp.square(jax.nn.relu(jnp.einsum('bsd,df->bsf', hn, w_ff1[layer])))
        h = h + jnp.einsum('bsf,fd->bsd', ff, w_ff2[layer])
    return rmsnorm(h, norm_final_g)
```

```python
import functools

import numpy as np
import jax
import jax.numpy as jnp
from jax import lax
from jax.experimental import pallas as pl
from jax.experimental.pallas import tpu as pltpu

F32 = jnp.float32
BF16 = jnp.bfloat16

EPS = 1e-6
NEG_BIG = -1e30
GROUP_WIDTH = 512
LANES = 128
N_HEADS = 4
POOL_WINDOWS = (2, 4, 8, 16)
CHUNK = 128
RET_QK = 64
ROPE_BASE = 10000.0
N_MAIN = 9 * GROUP_WIDTH
V7X_VMEM_LIMIT = 56 * 1024 * 1024


def _cparams(*sem):
    return pltpu.CompilerParams(dimension_semantics=sem, vmem_limit_bytes=V7X_VMEM_LIMIT)


def _resident(shape):
    nd = len(shape)
    return pl.BlockSpec(shape, lambda *_: (0,) * nd, pipeline_mode=pl.Buffered(1))


def _rms_scale(x):
    return lax.rsqrt(jnp.mean(x * x, axis=-1, keepdims=True) + EPS)


def _inproj_kernel(h_ref, g_ref, w_ref, wf_ref, proj_ref, flog_ref):
    x = h_ref[...]
    xn = (x * _rms_scale(x) * g_ref[...]).astype(BF16)
    for c in range(N_MAIN // GROUP_WIDTH):
        cols = slice(c * GROUP_WIDTH, (c + 1) * GROUP_WIDTH)
        proj_ref[:, cols] = jnp.dot(xn, w_ref[:, cols], preferred_element_type=F32).astype(BF16)
    flog_ref[...] = lax.dot_general(wf_ref[...], xn, (((1,), (1,)), ((), ())), preferred_element_type=F32)


def _inproj(h, g, w_main, wf_t, tm):
    n, d = h.shape
    return pl.pallas_call(
        _inproj_kernel,
        grid=(n // tm,),
        in_specs=[
            pl.BlockSpec((tm, d), lambda i: (i, 0)),
            _resident((1, d)),
            _resident((d, N_MAIN)),
            _resident((8, d)),
        ],
        out_specs=[
            pl.BlockSpec((tm, N_MAIN), lambda i: (i, 0)),
            pl.BlockSpec((8, tm), lambda i: (0, i)),
        ],
        out_shape=[jax.ShapeDtypeStruct((n, N_MAIN), BF16), jax.ShapeDtypeStruct((8, n), F32)],
        compiler_params=_cparams("parallel"),
        name="inproj",
    )(h, g, w_main, wf_t)


def _pool_kernel(a_ref, pw_ref, ps_ref, o_ref):
    s_len = a_ref.shape[0]
    row = lax.broadcasted_iota(jnp.int32, (s_len, LANES), 0)
    for gi, w in enumerate(POOL_WINDOWS):
        cols = slice(gi * LANES, (gi + 1) * LANES)
        a = a_ref[:, cols].astype(F32)
        tot = a
        sh = 1
        while sh < w:
            tot = tot + jnp.where(row >= sh, pltpu.roll(tot, sh, axis=0), 0.0)
            sh *= 2
        cnt = jnp.minimum(row + 1, w).astype(F32)
        p = tot / cnt - a
        y = jnp.dot(p.astype(BF16), pw_ref[gi].astype(BF16), preferred_element_type=F32)
        o_ref[:, cols] = (y * ps_ref[:, cols]).astype(BF16)


def _pool(proj, pool_w, pool_scale, batch, seq):
    return pl.pallas_call(
        _pool_kernel,
        grid=(batch,),
        in_specs=[
            pl.BlockSpec((seq, GROUP_WIDTH), lambda b: (b, 0)),
            _resident(pool_w.shape),
            _resident((1, GROUP_WIDTH)),
        ],
        out_specs=pl.BlockSpec((seq, GROUP_WIDTH), lambda b: (b, 0)),
        out_shape=jax.ShapeDtypeStruct((batch * seq, GROUP_WIDTH), BF16),
        compiler_params=_cparams("parallel"),
        name="pool",
    )(proj, pool_w, pool_scale)


def _head_norm(x):
    mu = jnp.mean(x, axis=-1, keepdims=True)
    d = x - mu
    var = jnp.mean(d * d, axis=-1, keepdims=True)
    return d * lax.rsqrt(var + EPS)


def _sgu_kernel(u_ref, v_ref, ng_ref, w_ref, bias_ref, o_ref, *, n_chunks):
    r_i = lax.broadcasted_iota(jnp.int32, (CHUNK, CHUNK), 0)
    c_i = lax.broadcasted_iota(jnp.int32, (CHUNK, CHUNK), 1)
    w_causal = [jnp.where(c_i <= r_i, w_ref[h], 0.0).astype(BF16) for h in range(N_HEADS)]
    for ci in range(n_chunks):
        rows = slice(ci * CHUNK, (ci + 1) * CHUNK)
        u = jax.nn.gelu(u_ref[rows, :].astype(F32))
        v = jax.nn.gelu(v_ref[rows, :].astype(F32))
        zs = []
        for h in range(N_HEADS):
            cols = slice(h * LANES, (h + 1) * LANES)
            vn = (_head_norm(v[:, cols]) * ng_ref[:, cols]).astype(BF16)
            zs.append(jnp.dot(w_causal[h], vn, preferred_element_type=F32))
        z = jnp.concatenate(zs, axis=1) + bias_ref[...]
        o_ref[rows, :] = (u * z).astype(BF16)


def _sgu(proj, norm_g, w_s, bias_tab, n_rows, rows_per_step):
    n_chunks = rows_per_step // CHUNK
    return pl.pallas_call(
        functools.partial(_sgu_kernel, n_chunks=n_chunks),
        grid=(n_rows // rows_per_step,),
        in_specs=[
            pl.BlockSpec((rows_per_step, GROUP_WIDTH), lambda i: (i, 1)),
            pl.BlockSpec((rows_per_step, GROUP_WIDTH), lambda i: (i, 2)),
            _resident((1, GROUP_WIDTH)),
            _resident(w_s.shape),
            _resident((CHUNK, GROUP_WIDTH)),
        ],
        out_specs=pl.BlockSpec((rows_per_step, GROUP_WIDTH), lambda i: (i, 0)),
        out_shape=jax.ShapeDtypeStruct((n_rows, GROUP_WIDTH), BF16),
        compiler_params=_cparams("parallel"),
        name="sgu",
    )(proj, proj, norm_g, w_s, bias_tab)


def _fox_kernel(flog_ref, bf_ref, q_ref, k_ref, v_ref, o_ref, c_scr, *, tile, scale):
    h = pl.program_id(1)
    qi = pl.program_id(2)
    s_len = k_ref.shape[0]

    @pl.when((h == 0) & (qi == 0))
    def _():
        x = flog_ref[...] + bf_ref[:, 0:1]
        c = jnp.minimum(x, 0.0) - jnp.log1p(jnp.exp(-jnp.abs(x)))
        lane = lax.broadcasted_iota(jnp.int32, c.shape, 1)
        sh = 1
        while sh < s_len:
            c = c + jnp.where(lane >= sh, pltpu.roll(c, sh, axis=1), 0.0)
            sh *= 2
        for j in range(s_len // tile):
            c_scr[j] = c[:, j * tile:(j + 1) * tile]

    q = (q_ref[...].astype(F32) * scale).astype(BF16)

    def block(j, carry, diagonal):
        m, l, acc = carry
        start = pl.multiple_of(j * tile, tile)
        kb = k_ref[pl.ds(start, tile), :]
        vb = v_ref[pl.ds(start, tile), :]
        s = lax.dot_general(q, kb, (((1,), (1,)), ((), ())), preferred_element_type=F32)
        s = s - c_scr[j, pl.ds(h, 1), :]
        if diagonal:
            r_i = lax.broadcasted_iota(jnp.int32, s.shape, 0)
            c_i = lax.broadcasted_iota(jnp.int32, s.shape, 1)
            s = jnp.where(c_i <= r_i, s, NEG_BIG)
        m_new = jnp.maximum(m, jnp.max(s, axis=-1, keepdims=True))
        alpha = jnp.exp(m - m_new)
        p = jnp.exp(s - m_new)
        l = alpha * l + jnp.sum(p, axis=-1, keepdims=True)
        acc = alpha * acc + jnp.dot(p.astype(BF16), vb, preferred_element_type=F32)
        return m_new, l, acc

    init = (jnp.full((tile, 1), NEG_BIG, F32), jnp.zeros((tile, 1), F32), jnp.zeros((tile, LANES), F32))
    carry = lax.fori_loop(0, qi, lambda j, cr: block(j, cr, False), init)
    _, l, acc = block(qi, carry, True)
    o_ref[...] = (acc / l).astype(BF16)


def _fox(proj, flog_t, bf_tab, batch, seq, tile):
    nq = seq // tile
    hpg = GROUP_WIDTH // LANES
    return pl.pallas_call(
        functools.partial(_fox_kernel, tile=tile, scale=LANES ** -0.5),
        grid=(batch, N_HEADS, nq),
        in_specs=[
            pl.BlockSpec((8, seq), lambda b, h, i: (0, b)),
            _resident((8, LANES)),
            pl.BlockSpec((tile, LANES), lambda b, h, i: (b * nq + i, 3 * hpg + h)),
            pl.BlockSpec((seq, LANES), lambda b, h, i: (b, 4 * hpg + h)),
            pl.BlockSpec((seq, LANES), lambda b, h, i: (b, 5 * hpg + h)),
        ],
        out_specs=pl.BlockSpec((tile, LANES), lambda b, h, i: (b * nq + i, h)),
        out_shape=jax.ShapeDtypeStruct((batch * seq, GROUP_WIDTH), BF16),
        scratch_shapes=[pltpu.VMEM((seq // tile, 8, tile), F32)],
        compiler_params=_cparams("parallel", "arbitrary", "arbitrary"),
        name="fox",
    )(flog_t, bf_tab, proj, proj, proj)


def _ret_kernel(q_ref, k_ref, v_ref, g_ref, cos_ref, sin_ref, hm_ref, zeta_ref, decay_ref, xi_ref,
                cd_ref, bd_ref, ng_ref, o_ref, state_scr, *, n_chunks):
    @pl.when(pl.program_id(1) == 0)
    def _():
        state_scr[...] = jnp.zeros_like(state_scr)

    def rope(x, cos, sin):
        x1, x2 = x[:, :LANES], x[:, LANES:]
        return jnp.concatenate([x1 * cos - x2 * sin, x1 * sin + x2 * cos], axis=1)

    for ci in range(n_chunks):
        rows = slice(ci * CHUNK, (ci + 1) * CHUNK)
        cos, sin = cos_ref[rows, :], sin_ref[rows, :]
        qr = rope(q_ref[rows, :].astype(F32), cos, sin)
        kr = rope(k_ref[rows, :].astype(F32), cos, sin) * (RET_QK ** -0.5)
        v = v_ref[rows, :]
        qb = qr.astype(BF16)
        k_heads = jnp.concatenate([(kr * hm_ref[h:h + 1, :]).astype(BF16) for h in range(N_HEADS)], axis=0)
        s = lax.dot_general(qb, k_heads, (((1,), (1,)), ((), ())), preferred_element_type=F32)
        sb = (s * decay_ref[...]).astype(BF16)
        y_intra = jnp.concatenate(
            [jnp.dot(sb[:, h * LANES:(h + 1) * LANES], v[:, h * LANES:(h + 1) * LANES], preferred_element_type=F32)
             for h in range(N_HEADS)], axis=1)
        state = state_scr[...]
        y_cross = jnp.dot(qb, state.astype(BF16), preferred_element_type=F32) * xi_ref[...]
        kz_t = jnp.transpose(kr * zeta_ref[...]).astype(BF16)
        upd = jnp.dot(kz_t, v, preferred_element_type=F32)
        state_scr[...] = state * cd_ref[...] + upd * bd_ref[...]
        y = y_intra + y_cross
        yn = jnp.concatenate([_head_norm(y[:, h * LANES:(h + 1) * LANES]) for h in range(N_HEADS)], axis=1)
        g = g_ref[rows, :].astype(F32)
        o_ref[rows, :] = (jax.nn.silu(g) * (yn * ng_ref[...])).astype(BF16)


def _ret_tables(seq):
    half = RET_QK // 2
    inv = np.exp(-(np.arange(half, dtype=np.float32) / half) * np.float32(np.log(ROPE_BASE))).astype(np.float32)
    ang = np.arange(seq, dtype=np.float32)[:, None] * inv[None, :]
    cos = np.tile(np.cos(ang).astype(np.float32), (1, N_HEADS))
    sin = np.tile(np.sin(ang).astype(np.float32), (1, N_HEADS))
    gamma = (1.0 - np.exp((-5.0 - np.arange(N_HEADS, dtype=np.float32)) * np.float32(np.log(2.0)))).astype(np.float32)
    log_gamma = np.log(gamma).astype(np.float32)
    l = np.arange(CHUNK, dtype=np.float32)
    diff = l[:, None] - l[None, :]
    decay = np.where(diff[None] >= 0, np.exp(np.maximum(diff, 0.0)[None] * log_gamma[:, None, None]), 0.0)
    decay = np.transpose(decay, (1, 0, 2)).reshape(CHUNK, N_HEADS * CHUNK).astype(np.float32)
    xi = np.repeat(np.exp((l + 1.0)[:, None] * log_gamma[None, :]), LANES, axis=1).astype(np.float32)
    lane_head = np.tile(np.repeat(np.arange(N_HEADS), half), 2)
    zeta = np.exp((CHUNK - 1.0 - l)[:, None] * log_gamma[lane_head][None, :]).astype(np.float32)
    hm = np.zeros((8, 2 * LANES), np.float32)
    for h in range(N_HEADS):
        hm[h] = lane_head == h
    col_head = np.repeat(np.arange(N_HEADS), LANES)
    cd = np.exp(CHUNK * log_gamma)[col_head][None, :].astype(np.float32)
    bd = (lane_head[:, None] == col_head[None, :]).astype(np.float32)
    return tuple(jnp.asarray(t) for t in (cos, sin, hm, zeta, decay, xi, cd, bd))


def _ret(proj, norm_g, tables, batch, seq, rows_per_step):
    cos, sin, hm, zeta, decay, xi, cd, bd = tables
    n_chunks = rows_per_step // CHUNK
    steps = seq // rows_per_step
    half_blk = rows_per_step, 2 * LANES
    full_blk = rows_per_step, GROUP_WIDTH
    return pl.pallas_call(
        functools.partial(_ret_kernel, n_chunks=n_chunks),
        grid=(batch, steps),
        in_specs=[
            pl.BlockSpec(half_blk, lambda b, i: (b * steps + i, 12)),
            pl.BlockSpec(half_blk, lambda b, i: (b * steps + i, 13)),
            pl.BlockSpec(full_blk, lambda b, i: (b * steps + i, 7)),
            pl.BlockSpec(full_blk, lambda b, i: (b * steps + i, 8)),
            pl.BlockSpec((rows_per_step, LANES), lambda b, i: (i, 0)),
            pl.BlockSpec((rows_per_step, LANES), lambda b, i: (i, 0)),
            _resident(hm.shape), _resident(zeta.shape), _resident(decay.shape), _resident(xi.shape),
            _resident(cd.shape), _resident(bd.shape), _resident((1, GROUP_WIDTH)),
        ],
        out_specs=pl.BlockSpec(full_blk, lambda b, i: (b * steps + i, 0)),
        out_shape=jax.ShapeDtypeStruct((batch * seq, GROUP_WIDTH), BF16),
        scratch_shapes=[pltpu.VMEM((2 * LANES, GROUP_WIDTH), F32)],
        compiler_params=_cparams("parallel", "arbitrary"),
        name="retention",
    )(proj, proj, proj, proj, cos, sin, hm, zeta, decay, xi, cd, bd, norm_g)


def _outproj_kernel(h_ref, ya_ref, yb_ref, yc_ref, yd_ref, w_ref, o_ref, mix_scr):
    for gi, y_ref in enumerate((ya_ref, yb_ref, yc_ref, yd_ref)):
        mix_scr[:, gi * GROUP_WIDTH:(gi + 1) * GROUP_WIDTH] = y_ref[...]
    mix = mix_scr[...]
    for c in range(o_ref.shape[1] // GROUP_WIDTH):
        cols = slice(c * GROUP_WIDTH, (c + 1) * GROUP_WIDTH)
        o_ref[:, cols] = h_ref[:, cols] + jnp.dot(mix, w_ref[:, cols], preferred_element_type=F32)


def _outproj(h, ys, w_out, tm):
    n, d = h.shape
    y_spec = pl.BlockSpec((tm, GROUP_WIDTH), lambda i: (i, 0))
    return pl.pallas_call(
        _outproj_kernel,
        grid=(n // tm,),
        in_specs=[pl.BlockSpec((tm, d), lambda i: (i, 0)), y_spec, y_spec, y_spec, y_spec, _resident(w_out.shape)],
        out_specs=pl.BlockSpec((tm, d), lambda i: (i, 0)),
        out_shape=jax.ShapeDtypeStruct((n, d), F32),
        scratch_shapes=[pltpu.VMEM((tm, N_HEADS * GROUP_WIDTH), BF16)],
        compiler_params=_cparams("parallel"),
        name="outproj",
    )(h, *ys, w_out)


def _mlp_kernel(h_ref, g_ref, w1_ref, w2_ref, gf_ref, o_ref, hn_scr, *, final_norm):
    f = pl.program_id(1)

    @pl.when(f == 0)
    def _():
        x = h_ref[...]
        hn_scr[...] = (x * _rms_scale(x) * g_ref[...]).astype(BF16)
        o_ref[...] = x

    a = jnp.dot(hn_scr[...], w1_ref[...], preferred_element_type=F32)
    a = jnp.square(jnp.maximum(a, 0.0)).astype(BF16)
    o_ref[...] += jnp.dot(a, w2_ref[...], preferred_element_type=F32)

    if final_norm:
        @pl.when(f == pl.num_programs(1) - 1)
        def _():
            y = o_ref[...]
            o_ref[...] = y * _rms_scale(y) * gf_ref[...]


def _mlp(h, g, w1, w2, g_final, tm, tf, final_norm):
    n, d = h.shape
    d_ff = w1.shape[1]
    return pl.pallas_call(
        functools.partial(_mlp_kernel, final_norm=final_norm),
        grid=(n // tm, d_ff // tf),
        in_specs=[
            pl.BlockSpec((tm, d), lambda i, f: (i, 0)),
            _resident((1, d)),
            pl.BlockSpec((d, tf), lambda i, f: (0, f)),
            pl.BlockSpec((tf, d), lambda i, f: (f, 0)),
            _resident((1, d)),
        ],
        out_specs=pl.BlockSpec((tm, d), lambda i, f: (i, 0)),
        out_shape=jax.ShapeDtypeStruct((n, d), F32),
        scratch_shapes=[pltpu.VMEM((tm, d), BF16)],
        compiler_params=_cparams("parallel", "arbitrary"),
        name="mlp",
    )(h, g, w1, w2, g_final)


def _split_heads_to_slabs(w):
    d = w.shape[0]
    half = RET_QK // 2
    return w.reshape(d, N_HEADS, 2, half).transpose(0, 2, 1, 3).reshape(d, N_HEADS * RET_QK)


def _prep_w_in(w):
    gw = GROUP_WIDTH
    f0 = 6 * gw
    q0 = f0 + N_HEADS
    k0 = q0 + N_HEADS * RET_QK
    v0 = k0 + N_HEADS * RET_QK
    main = jnp.concatenate(
        [w[:, :f0], _split_heads_to_slabs(w[:, q0:k0]), _split_heads_to_slabs(w[:, k0:v0]), w[:, v0:v0 + 2 * gw]],
        axis=1).astype(BF16)
    wf_t = jnp.zeros((8, w.shape[0]), BF16).at[:N_HEADS].set(w[:, f0:q0].T.astype(BF16))
    return main, wf_t


def kernel(x, norm_mix_g, w_in, fox_b_f, pool_w, pool_scale, sgu_norm_g, sgu_w_s, sgu_b, ret_norm_g, w_out,
           norm_mlp_g, w_ff1, w_ff2, norm_final_g):
    batch, seq, d = x.shape
    depth = w_in.shape[0]
    n = batch * seq
    tm = min(512, n)
    rows_per_step = min(512, seq)
    fox_tile = min(256, seq)
    tables = _ret_tables(seq)

    h = x.reshape(n, d)
    g_final = norm_final_g.reshape(1, d)
    for layer in range(depth):
        w_main, wf_t = _prep_w_in(w_in[layer])
        proj, flog_t = _inproj(h, norm_mix_g[layer].reshape(1, d), w_main, wf_t, tm)

        y_a = _pool(proj, pool_w[layer], pool_scale[layer].reshape(1, GROUP_WIDTH), batch, seq)
        sgu_bias = jnp.repeat(jnp.transpose(sgu_b[layer]), LANES, axis=1)
        y_b = _sgu(proj, sgu_norm_g[layer].reshape(1, GROUP_WIDTH), sgu_w_s[layer], sgu_bias, n, rows_per_step)
        bf_tab = jnp.zeros((8, LANES), F32).at[:N_HEADS].set(jnp.broadcast_to(fox_b_f[layer][:, None], (N_HEADS, LANES)))
        y_c = _fox(proj, flog_t, bf_tab, batch, seq, fox_tile)
        y_d = _ret(proj, ret_norm_g[layer].reshape(1, GROUP_WIDTH), tables, batch, seq, rows_per_step)

        h = _outproj(h, (y_a, y_b, y_c, y_d), w_out[layer].astype(BF16), tm)
        h = _mlp(h, norm_mlp_g[layer].reshape(1, d), w_ff1[layer].astype(BF16), w_ff2[layer].astype(BF16), g_final,
                 tm, 512, final_norm=(layer == depth - 1))
    return h.reshape(batch, seq, d)
```

```python
import functools

import numpy as np
import jax
import jax.numpy as jnp
from jax import lax
from jax.experimental import pallas as pl
from jax.experimental.pallas import tpu as pltpu

F32 = jnp.float32
BF16 = jnp.bfloat16

EPS = 1e-6
NEG_BIG = -1e30
GROUP_WIDTH = 512
LANES = 128
N_HEADS = 4
POOL_WINDOWS = (2, 4, 8, 16)
CHUNK = 128
RET_QK = 64
ROPE_BASE = 10000.0
N_MAIN = 9 * GROUP_WIDTH
F_COL0 = 6 * GROUP_WIDTH
TAIL_COLS = N_MAIN - F_COL0
TAIL_WINDOW = -(-(N_HEADS + TAIL_COLS) // LANES) * LANES
V7X_VMEM_LIMIT = 56 * 1024 * 1024
FOX_TQ, FOX_TK, FOX_TD, FOX_HEADS_PER_STEP = 512, 512, 512, 4


def _cparams(*sem):
    return pltpu.CompilerParams(dimension_semantics=sem, vmem_limit_bytes=V7X_VMEM_LIMIT)


def _resident(shape):
    nd = len(shape)
    return pl.BlockSpec(shape, lambda *_: (0,) * nd, pipeline_mode=pl.Buffered(1))


def _rms_scale(x):
    return lax.rsqrt(jnp.mean(x * x, axis=-1, keepdims=True) + EPS)


def _retention_slab_permutation():
    n = 2 * N_HEADS * RET_QK
    src_row = lax.broadcasted_iota(jnp.int32, (n, n), 0)
    j = lax.broadcasted_iota(jnp.int32, (n, n), 1)
    part, jj = j >> 8, j & 255
    half, head, freq = jj >> 7, (jj & 127) >> 5, jj & 31
    src = (part << 8) + (head << 6) + (half << 5) + freq
    return jnp.where(src_row == src, 1.0, 0.0).astype(BF16)


def _inproj_kernel(h_ref, g_ref, w_ref, proj_ref, flog_ref, w_tail_scr, wf_scr):
    d = w_ref.shape[0]

    @pl.when(pl.program_id(0) == 0)
    def _():
        perm = _retention_slab_permutation()
        rb = 256
        for r in range(d // rb):
            rows = slice(r * rb, (r + 1) * rb)
            win = w_ref[rows, F_COL0:F_COL0 + TAIL_WINDOW].astype(F32)
            tail = pltpu.roll(win, TAIL_WINDOW - N_HEADS, axis=1)[:, :TAIL_COLS].astype(BF16)
            qk = jnp.dot(tail[:, :GROUP_WIDTH], perm, preferred_element_type=F32)
            w_tail_scr[rows, :GROUP_WIDTH] = qk.astype(BF16)
            w_tail_scr[rows, GROUP_WIDTH:] = tail[:, GROUP_WIDTH:]
        wf_t = jnp.transpose(w_ref[:, F_COL0:F_COL0 + LANES].astype(F32))[:8, :]
        keep = lax.broadcasted_iota(jnp.int32, wf_t.shape, 0) < N_HEADS
        wf_scr[...] = jnp.where(keep, wf_t, 0.0).astype(BF16)

    x = h_ref[...]
    xn = (x * _rms_scale(x) * g_ref[...]).astype(BF16)
    for c in range(N_MAIN // GROUP_WIDTH):
        cols = slice(c * GROUP_WIDTH, (c + 1) * GROUP_WIDTH)
        w = w_ref[:, cols] if c < F_COL0 // GROUP_WIDTH else w_tail_scr[:, c * GROUP_WIDTH - F_COL0:(c + 1) * GROUP_WIDTH - F_COL0]
        proj_ref[:, cols] = jnp.dot(xn, w, preferred_element_type=F32).astype(BF16)
    flog_ref[...] = lax.dot_general(wf_scr[...], xn, (((1,), (1,)), ((), ())), preferred_element_type=F32)


def _inproj(h, g, w_in_b, layer, tm):
    n, d = h.shape
    w_spec = pl.BlockSpec((None,) + w_in_b.shape[1:], lambda i: (layer, 0, 0), pipeline_mode=pl.Buffered(1))
    return pl.pallas_call(
        _inproj_kernel,
        grid=(n // tm,),
        in_specs=[pl.BlockSpec((tm, d), lambda i: (i, 0)), _resident((1, d)), w_spec],
        out_specs=[
            pl.BlockSpec((tm, N_MAIN), lambda i: (i, 0)),
            pl.BlockSpec((8, tm), lambda i: (0, i)),
        ],
        out_shape=[jax.ShapeDtypeStruct((n, N_MAIN), BF16), jax.ShapeDtypeStruct((8, n), F32)],
        scratch_shapes=[pltpu.VMEM((d, TAIL_COLS), BF16), pltpu.VMEM((8, d), BF16)],
        compiler_params=_cparams("arbitrary"),
        name="inproj",
    )(h, g, w_in_b)


def _pool_kernel(a_ref, pw_ref, ps_ref, o_ref):
    s_len = a_ref.shape[0]
    row = lax.broadcasted_iota(jnp.int32, (s_len, LANES), 0)
    for gi, w in enumerate(POOL_WINDOWS):
        cols = slice(gi * LANES, (gi + 1) * LANES)
        a = a_ref[:, cols].astype(F32)
        tot = a
        sh = 1
        while sh < w:
            tot = tot + jnp.where(row >= sh, pltpu.roll(tot, sh, axis=0), 0.0)
            sh *= 2
        cnt = jnp.minimum(row + 1, w).astype(F32)
        p = tot / cnt - a
        y = jnp.dot(p.astype(BF16), pw_ref[gi].astype(BF16), preferred_element_type=F32)
        o_ref[:, cols] = (y * ps_ref[:, cols]).astype(BF16)


def _pool(proj, pool_w, pool_scale, batch, seq):
    return pl.pallas_call(
        _pool_kernel,
        grid=(batch,),
        in_specs=[
            pl.BlockSpec((seq, GROUP_WIDTH), lambda b: (b, 0)),
            _resident(pool_w.shape),
            _resident((1, GROUP_WIDTH)),
        ],
        out_specs=pl.BlockSpec((seq, GROUP_WIDTH), lambda b: (b, 0)),
        out_shape=jax.ShapeDtypeStruct((batch * seq, GROUP_WIDTH), BF16),
        compiler_params=_cparams("parallel"),
        name="pool",
    )(proj, pool_w, pool_scale)


def _head_norm(x):
    mu = jnp.mean(x, axis=-1, keepdims=True)
    d = x - mu
    var = jnp.mean(d * d, axis=-1, keepdims=True)
    return d * lax.rsqrt(var + EPS)


def _sgu_kernel(u_ref, v_ref, ng_ref, w_ref, bias_ref, o_ref, *, n_chunks):
    r_i = lax.broadcasted_iota(jnp.int32, (CHUNK, CHUNK), 0)
    c_i = lax.broadcasted_iota(jnp.int32, (CHUNK, CHUNK), 1)
    w_causal = [jnp.where(c_i <= r_i, w_ref[h], 0.0).astype(BF16) for h in range(N_HEADS)]
    for ci in range(n_chunks):
        rows = slice(ci * CHUNK, (ci + 1) * CHUNK)
        u = jax.nn.gelu(u_ref[rows, :].astype(F32))
        v = jax.nn.gelu(v_ref[rows, :].astype(F32))
        zs = []
        for h in range(N_HEADS):
            cols = slice(h * LANES, (h + 1) * LANES)
            vn = (_head_norm(v[:, cols]) * ng_ref[:, cols]).astype(BF16)
            zs.append(jnp.dot(w_causal[h], vn, preferred_element_type=F32))
        z = jnp.concatenate(zs, axis=1) + bias_ref[...]
        o_ref[rows, :] = (u * z).astype(BF16)


def _sgu(proj, norm_g, w_s, bias_tab, n_rows, rows_per_step):
    n_chunks = rows_per_step // CHUNK
    return pl.pallas_call(
        functools.partial(_sgu_kernel, n_chunks=n_chunks),
        grid=(n_rows // rows_per_step,),
        in_specs=[
            pl.BlockSpec((rows_per_step, GROUP_WIDTH), lambda i: (i, 1)),
            pl.BlockSpec((rows_per_step, GROUP_WIDTH), lambda i: (i, 2)),
            _resident((1, GROUP_WIDTH)),
            _resident(w_s.shape),
            _resident((CHUNK, GROUP_WIDTH)),
        ],
        out_specs=pl.BlockSpec((rows_per_step, GROUP_WIDTH), lambda i: (i, 0)),
        out_shape=jax.ShapeDtypeStruct((n_rows, GROUP_WIDTH), BF16),
        compiler_params=_cparams("parallel"),
        name="sgu",
    )(proj, proj, norm_g, w_s, bias_tab)


def _fox_kernel(flog_ref, bf_ref, q_ref, k_ref, v_ref, o_ref, c_scr, *, tq, tk, td, hps, scale):
    hg = pl.program_id(1)
    qi = pl.program_id(2)
    s_len = k_ref.shape[0]
    log2e = float(np.log2(np.e))

    @pl.when((hg == 0) & (qi == 0))
    def _():
        x = flog_ref[...] + bf_ref[:, 0:1]
        c = jnp.minimum(x, 0.0) - jnp.log1p(jnp.exp(-jnp.abs(x)))
        lane = lax.broadcasted_iota(jnp.int32, c.shape, 1)
        sh = 1
        while sh < s_len:
            c = c + jnp.where(lane >= sh, pltpu.roll(c, sh, axis=1), 0.0)
            sh *= 2
        c = c * log2e
        for j in range(s_len // td):
            c_scr[j] = c[:, j * td:(j + 1) * td]

    heads = [slice(h * LANES, (h + 1) * LANES) for h in range(hps)]
    qs = [(q_ref[:, hs].astype(F32) * (scale * log2e)).astype(BF16) for hs in heads]
    row0 = qi * tq

    def update(state, q, kb, vb, ck, n_masked):
        m, l, acc = state
        s = lax.dot_general(q, kb, (((1,), (1,)), ((), ())), preferred_element_type=F32) - ck
        if n_masked:
            top = s[:n_masked]
            r_i = lax.broadcasted_iota(jnp.int32, top.shape, 0)
            c_i = lax.broadcasted_iota(jnp.int32, top.shape, 1)
            top = jnp.where(c_i <= r_i, top, NEG_BIG)
            s = top if n_masked == s.shape[0] else jnp.concatenate([top, s[n_masked:]], axis=0)
        m_new = jnp.maximum(m, jnp.max(s, axis=-1, keepdims=True))
        alpha = jnp.exp2(m - m_new)
        p = jnp.exp2(s - m_new)
        l = alpha * l + jnp.sum(p, axis=-1, keepdims=True)
        acc = alpha * acc + jnp.dot(p.astype(BF16), vb, preferred_element_type=F32)
        return m_new, l, acc

    def bias_row(h, key_start, width):
        first = key_start // td
        parts = [c_scr[first + i, pl.ds(hg * hps + h, 1), :] for i in range(width // td)]
        return parts[0] if len(parts) == 1 else jnp.concatenate(parts, axis=1)

    def full_block(j, carry):
        start = pl.multiple_of(j * tk, tk)
        return tuple(
            update(carry[h], qs[h], k_ref[pl.ds(start, tk), hs], v_ref[pl.ds(start, tk), hs], bias_row(h, start, tk), 0)
            for h, hs in enumerate(heads))

    carry = tuple((jnp.full((tq, 1), NEG_BIG, F32), jnp.zeros((tq, 1), F32), jnp.zeros((tq, LANES), F32))
                  for _ in heads)
    carry = lax.fori_loop(0, row0 // tk, full_block, carry)

    for d in range(tq // td):
        r0 = d * td
        start = pl.multiple_of(row0 + r0, td)
        new = []
        for h, hs in enumerate(heads):
            m, l, acc = carry[h]
            sub = update((m[r0:], l[r0:], acc[r0:]), qs[h][r0:], k_ref[pl.ds(start, td), hs],
                         v_ref[pl.ds(start, td), hs], bias_row(h, start, td), td)
            new.append(tuple(jnp.concatenate([old[:r0], upd], axis=0) if r0 else upd
                             for old, upd in zip((m, l, acc), sub)))
        carry = tuple(new)

    for h, hs in enumerate(heads):
        _, l, acc = carry[h]
        o_ref[:, hs] = (acc / l).astype(BF16)


def _fox(proj, flog_t, bf_tab, batch, seq, tq, tk, td, hps):
    assert tq % td == 0 and tk % td == 0 and tq % tk == 0
    nq = seq // tq
    width = hps * LANES
    per_piece = GROUP_WIDTH // width
    return pl.pallas_call(
        functools.partial(_fox_kernel, tq=tq, tk=tk, td=td, hps=hps, scale=LANES ** -0.5),
        grid=(batch, per_piece, nq),
        in_specs=[
            pl.BlockSpec((8, seq), lambda b, g, i: (0, b)),
            _resident((8, LANES)),
            pl.BlockSpec((tq, width), lambda b, g, i: (b * nq + i, 3 * per_piece + g)),
            pl.BlockSpec((seq, width), lambda b, g, i: (b, 4 * per_piece + g)),
            pl.BlockSpec((seq, width), lambda b, g, i: (b, 5 * per_piece + g)),
        ],
        out_specs=pl.BlockSpec((tq, width), lambda b, g, i: (b * nq + i, g)),
        out_shape=jax.ShapeDtypeStruct((batch * seq, GROUP_WIDTH), BF16),
        scratch_shapes=[pltpu.VMEM((seq // td, 8, td), F32)],
        compiler_params=_cparams("parallel", "arbitrary", "arbitrary"),
        name="fox",
    )(flog_t, bf_tab, proj, proj, proj)


def _ret_kernel(q_ref, k_ref, v_ref, g_ref, cos_ref, sin_ref, hm_ref, zeta_ref, decay_ref, xi_ref,
                cd_ref, bd_ref, ng_ref, o_ref, state_scr, *, n_chunks):
    @pl.when(pl.program_id(1) == 0)
    def _():
        state_scr[...] = jnp.zeros_like(state_scr)

    def rope(x, cos, sin):
        x1, x2 = x[:, :LANES], x[:, LANES:]
        return jnp.concatenate([x1 * cos - x2 * sin, x1 * sin + x2 * cos], axis=1)

    for ci in range(n_chunks):
        rows = slice(ci * CHUNK, (ci + 1) * CHUNK)
        cos, sin = cos_ref[rows, :], sin_ref[rows, :]
        qr = rope(q_ref[rows, :].astype(F32), cos, sin)
        kr = rope(k_ref[rows, :].astype(F32), cos, sin) * (RET_QK ** -0.5)
        v = v_ref[rows, :]
        qb = qr.astype(BF16)
        k_heads = jnp.concatenate([(kr * hm_ref[h:h + 1, :]).astype(BF16) for h in range(N_HEADS)], axis=0)
        s = lax.dot_general(qb, k_heads, (((1,), (1,)), ((), ())), preferred_element_type=F32)
        sb = (s * decay_ref[...]).astype(BF16)
        y_intra = jnp.concatenate(
            [jnp.dot(sb[:, h * LANES:(h + 1) * LANES], v[:, h * LANES:(h + 1) * LANES], preferred_element_type=F32)
             for h in range(N_HEADS)], axis=1)
        state = state_scr[...]
        y_cross = jnp.dot(qb, state.astype(BF16), preferred_element_type=F32) * xi_ref[...]
        kz_t = jnp.transpose(kr * zeta_ref[...]).astype(BF16)
        upd = jnp.dot(kz_t, v, preferred_element_type=F32)
        state_scr[...] = state * cd_ref[...] + upd * bd_ref[...]
        y = y_intra + y_cross
        yn = jnp.concatenate([_head_norm(y[:, h * LANES:(h + 1) * LANES]) for h in range(N_HEADS)], axis=1)
        g = g_ref[rows, :].astype(F32)
        o_ref[rows, :] = (jax.nn.silu(g) * (yn * ng_ref[...])).astype(BF16)


def _ret_tables(seq):
    half = RET_QK // 2
    inv = np.exp(-(np.arange(half, dtype=np.float32) / half) * np.float32(np.log(ROPE_BASE))).astype(np.float32)
    ang = np.arange(seq, dtype=np.float32)[:, None] * inv[None, :]
    cos = np.tile(np.cos(ang).astype(np.float32), (1, N_HEADS))
    sin = np.tile(np.sin(ang).astype(np.float32), (1, N_HEADS))
    gamma = (1.0 - np.exp((-5.0 - np.arange(N_HEADS, dtype=np.float32)) * np.float32(np.log(2.0)))).astype(np.float32)
    log_gamma = np.log(gamma).astype(np.float32)
    l = np.arange(CHUNK, dtype=np.float32)
    diff = l[:, None] - l[None, :]
    decay = np.where(diff[None] >= 0, np.exp(np.maximum(diff, 0.0)[None] * log_gamma[:, None, None]), 0.0)
    decay = np.transpose(decay, (1, 0, 2)).reshape(CHUNK, N_HEADS * CHUNK).astype(np.float32)
    xi = np.repeat(np.exp((l + 1.0)[:, None] * log_gamma[None, :]), LANES, axis=1).astype(np.float32)
    lane_head = np.tile(np.repeat(np.arange(N_HEADS), half), 2)
    zeta = np.exp((CHUNK - 1.0 - l)[:, None] * log_gamma[lane_head][None, :]).astype(np.float32)
    hm = np.zeros((8, 2 * LANES), np.float32)
    for h in range(N_HEADS):
        hm[h] = lane_head == h
    col_head = np.repeat(np.arange(N_HEADS), LANES)
    cd = np.exp(CHUNK * log_gamma)[col_head][None, :].astype(np.float32)
    bd = (lane_head[:, None] == col_head[None, :]).astype(np.float32)
    return tuple(jnp.asarray(t) for t in (cos, sin, hm, zeta, decay, xi, cd, bd))


def _ret(proj, norm_g, tables, batch, seq, rows_per_step):
    cos, sin, hm, zeta, decay, xi, cd, bd = tables
    n_chunks = rows_per_step // CHUNK
    steps = seq // rows_per_step
    half_blk = rows_per_step, 2 * LANES
    full_blk = rows_per_step, GROUP_WIDTH
    return pl.pallas_call(
        functools.partial(_ret_kernel, n_chunks=n_chunks),
        grid=(batch, steps),
        in_specs=[
            pl.BlockSpec(half_blk, lambda b, i: (b * steps + i, 12)),
            pl.BlockSpec(half_blk, lambda b, i: (b * steps + i, 13)),
            pl.BlockSpec(full_blk, lambda b, i: (b * steps + i, 7)),
            pl.BlockSpec(full_blk, lambda b, i: (b * steps + i, 8)),
            pl.BlockSpec((rows_per_step, LANES), lambda b, i: (i, 0)),
            pl.BlockSpec((rows_per_step, LANES), lambda b, i: (i, 0)),
            _resident(hm.shape), _resident(zeta.shape), _resident(decay.shape), _resident(xi.shape),
            _resident(cd.shape), _resident(bd.shape), _resident((1, GROUP_WIDTH)),
        ],
        out_specs=pl.BlockSpec(full_blk, lambda b, i: (b * steps + i, 0)),
        out_shape=jax.ShapeDtypeStruct((batch * seq, GROUP_WIDTH), BF16),
        scratch_shapes=[pltpu.VMEM((2 * LANES, GROUP_WIDTH), F32)],
        compiler_params=_cparams("parallel", "arbitrary"),
        name="retention",
    )(proj, proj, proj, proj, cos, sin, hm, zeta, decay, xi, cd, bd, norm_g)


def _outproj_kernel(h_ref, ya_ref, yb_ref, yc_ref, yd_ref, w_ref, o_ref, mix_scr):
    for gi, y_ref in enumerate((ya_ref, yb_ref, yc_ref, yd_ref)):
        mix_scr[:, gi * GROUP_WIDTH:(gi + 1) * GROUP_WIDTH] = y_ref[...]
    mix = mix_scr[...]
    for c in range(o_ref.shape[1] // GROUP_WIDTH):
        cols = slice(c * GROUP_WIDTH, (c + 1) * GROUP_WIDTH)
        o_ref[:, cols] = h_ref[:, cols] + jnp.dot(mix, w_ref[:, cols], preferred_element_type=F32)


def _outproj(h, ys, w_out, layer, tm):
    n, d = h.shape
    y_spec = pl.BlockSpec((tm, GROUP_WIDTH), lambda i: (i, 0))
    w_spec = pl.BlockSpec((None,) + w_out.shape[1:], lambda i: (layer, 0, 0), pipeline_mode=pl.Buffered(1))
    return pl.pallas_call(
        _outproj_kernel,
        grid=(n // tm,),
        in_specs=[pl.BlockSpec((tm, d), lambda i: (i, 0)), y_spec, y_spec, y_spec, y_spec, w_spec],
        out_specs=pl.BlockSpec((tm, d), lambda i: (i, 0)),
        out_shape=jax.ShapeDtypeStruct((n, d), F32),
        scratch_shapes=[pltpu.VMEM((tm, N_HEADS * GROUP_WIDTH), BF16)],
        compiler_params=_cparams("parallel"),
        name="outproj",
    )(h, *ys, w_out)


def _mlp_kernel(h_ref, g_ref, w1_ref, w2_ref, gf_ref, o_ref, hn_scr, *, final_norm, n_split):
    f = pl.program_id(1)
    sub = h_ref.shape[0] // n_split
    row_groups = [slice(r * sub, (r + 1) * sub) for r in range(n_split)]

    @pl.when(f == 0)
    def _():
        for rows in row_groups:
            x = h_ref[rows, :]
            hn_scr[rows, :] = (x * _rms_scale(x) * g_ref[...]).astype(BF16)
            o_ref[rows, :] = x

    for rows in row_groups:
        a = jnp.dot(hn_scr[rows, :], w1_ref[...], preferred_element_type=F32)
        a = jnp.square(jnp.maximum(a, 0.0)).astype(BF16)
        o_ref[rows, :] += jnp.dot(a, w2_ref[...], preferred_element_type=F32)

    if final_norm:
        @pl.when(f == pl.num_programs(1) - 1)
        def _():
            for rows in row_groups:
                y = o_ref[rows, :]
                o_ref[rows, :] = y * _rms_scale(y) * gf_ref[...]


def _mlp(h, g, w1, w2, g_final, layer, tm, tf, final_norm):
    n, d = h.shape
    d_ff = w1.shape[2]
    return pl.pallas_call(
        functools.partial(_mlp_kernel, final_norm=final_norm, n_split=max(1, tm // 512)),
        grid=(n // tm, d_ff // tf),
        in_specs=[
            pl.BlockSpec((tm, d), lambda i, f: (i, 0)),
            _resident((1, d)),
            pl.BlockSpec((None, d, tf), lambda i, f: (layer, 0, f)),
            pl.BlockSpec((None, tf, d), lambda i, f: (layer, f, 0)),
            _resident((1, d)),
        ],
        out_specs=pl.BlockSpec((tm, d), lambda i, f: (i, 0)),
        out_shape=jax.ShapeDtypeStruct((n, d), F32),
        scratch_shapes=[pltpu.VMEM((tm, d), BF16)],
        compiler_params=_cparams("parallel", "arbitrary"),
        name="mlp",
    )(h, g, w1, w2, g_final)


def kernel(x, norm_mix_g, w_in, fox_b_f, pool_w, pool_scale, sgu_norm_g, sgu_w_s, sgu_b, ret_norm_g, w_out,
           norm_mlp_g, w_ff1, w_ff2, norm_final_g):
    batch, seq, d = x.shape
    depth = w_in.shape[0]
    n = batch * seq
    tm = min(512, n)
    tm_mlp = min(1024, n)
    rows_per_step = min(512, seq)
    fox_tiles = (min(FOX_TQ, seq), min(FOX_TK, seq), min(FOX_TD, seq), FOX_HEADS_PER_STEP)
    tables = _ret_tables(seq)

    w_in_b = w_in.astype(BF16)
    w_out_b = w_out.astype(BF16)
    w_ff1_b = w_ff1.astype(BF16)
    w_ff2_b = w_ff2.astype(BF16)

    h = x.reshape(n, d)
    g_final = norm_final_g.reshape(1, d)
    for layer in range(depth):
        proj, flog_t = _inproj(h, norm_mix_g[layer].reshape(1, d), w_in_b, layer, tm)

        y_a = _pool(proj, pool_w[layer], pool_scale[layer].reshape(1, GROUP_WIDTH), batch, seq)
        sgu_bias = jnp.repeat(jnp.transpose(sgu_b[layer]), LANES, axis=1)
        y_b = _sgu(proj, sgu_norm_g[layer].reshape(1, GROUP_WIDTH), sgu_w_s[layer], sgu_bias, n, rows_per_step)
        bf_tab = jnp.zeros((8, LANES), F32).at[:N_HEADS].set(jnp.broadcast_to(fox_b_f[layer][:, None], (N_HEADS, LANES)))
        y_c = _fox(proj, flog_t, bf_tab, batch, seq, *fox_tiles)
        y_d = _ret(proj, ret_norm_g[layer].reshape(1, GROUP_WIDTH), tables, batch, seq, rows_per_step)

        h = _outproj(h, (y_a, y_b, y_c, y_d), w_out_b, layer, tm)
        h = _mlp(h, norm_mlp_g[layer].reshape(1, d), w_ff1_b, w_ff2_b, g_final, layer, tm_mlp, 512,
                 final_norm=(layer == depth - 1))
    return h.reshape(batch, seq, d)
```

```python
import functools

import numpy as np
import jax
import jax.numpy as jnp
from jax import lax
from jax.experimental import pallas as pl
from jax.experimental.pallas import tpu as pltpu

F32 = jnp.float32
BF16 = jnp.bfloat16

EPS = 1e-6
NEG_BIG = -1e30
GROUP_WIDTH = 512
LANES = 128
N_HEADS = 4
POOL_WINDOWS = (2, 4, 8, 16)
CHUNK = 128
RET_QK = 64
ROPE_BASE = 10000.0
N_MAIN = 9 * GROUP_WIDTH
F_COL0 = 6 * GROUP_WIDTH
TAIL_COLS = N_MAIN - F_COL0
TAIL_WINDOW = -(-(N_HEADS + TAIL_COLS) // LANES) * LANES
V7X_VMEM_LIMIT = 56 * 1024 * 1024
FOX_TQ, FOX_TK, FOX_TD, FOX_HEADS_PER_STEP = 512, 512, 512, 4


def _cparams(*sem):
    return pltpu.CompilerParams(dimension_semantics=sem, vmem_limit_bytes=V7X_VMEM_LIMIT)


def _resident(shape):
    nd = len(shape)
    return pl.BlockSpec(shape, lambda *_: (0,) * nd, pipeline_mode=pl.Buffered(1))


def _rms_scale(x):
    return lax.rsqrt(jnp.mean(x * x, axis=-1, keepdims=True) + EPS)


def _retention_slab_permutation():
    n = 2 * N_HEADS * RET_QK
    src_row = lax.broadcasted_iota(jnp.int32, (n, n), 0)
    j = lax.broadcasted_iota(jnp.int32, (n, n), 1)
    part, jj = j >> 8, j & 255
    half, head, freq = jj >> 7, (jj & 127) >> 5, jj & 31
    src = (part << 8) + (head << 6) + (half << 5) + freq
    return jnp.where(src_row == src, 1.0, 0.0).astype(BF16)


def _inproj_kernel(h_ref, g_ref, w_ref, proj_ref, flog_ref, w_tail_scr, wf_scr):
    d = w_ref.shape[0]

    @pl.when(pl.program_id(0) == 0)
    def _():
        perm = _retention_slab_permutation()
        rb = 256
        for r in range(d // rb):
            rows = slice(r * rb, (r + 1) * rb)
            win = w_ref[rows, F_COL0:F_COL0 + TAIL_WINDOW].astype(F32)
            tail = pltpu.roll(win, TAIL_WINDOW - N_HEADS, axis=1)[:, :TAIL_COLS].astype(BF16)
            qk = jnp.dot(tail[:, :GROUP_WIDTH], perm, preferred_element_type=F32)
            w_tail_scr[rows, :GROUP_WIDTH] = qk.astype(BF16)
            w_tail_scr[rows, GROUP_WIDTH:] = tail[:, GROUP_WIDTH:]
        wf_t = jnp.transpose(w_ref[:, F_COL0:F_COL0 + LANES].astype(F32))[:8, :]
        keep = lax.broadcasted_iota(jnp.int32, wf_t.shape, 0) < N_HEADS
        wf_scr[...] = jnp.where(keep, wf_t, 0.0).astype(BF16)

    x = h_ref[...]
    xn = (x * _rms_scale(x) * g_ref[...]).astype(BF16)
    for c in range(N_MAIN // GROUP_WIDTH):
        cols = slice(c * GROUP_WIDTH, (c + 1) * GROUP_WIDTH)
        w = w_ref[:, cols] if c < F_COL0 // GROUP_WIDTH else w_tail_scr[:, c * GROUP_WIDTH - F_COL0:(c + 1) * GROUP_WIDTH - F_COL0]
        proj_ref[:, cols] = jnp.dot(xn, w, preferred_element_type=F32).astype(BF16)
    flog_ref[...] = lax.dot_general(wf_scr[...], xn, (((1,), (1,)), ((), ())), preferred_element_type=F32)


def _inproj(h, g, w_in_b, layer, tm):
    n, d = h.shape
    w_spec = pl.BlockSpec((None,) + w_in_b.shape[1:], lambda i: (layer, 0, 0), pipeline_mode=pl.Buffered(1))
    return pl.pallas_call(
        _inproj_kernel,
        grid=(n // tm,),
        in_specs=[pl.BlockSpec((tm, d), lambda i: (i, 0)), _resident((1, d)), w_spec],
        out_specs=[
            pl.BlockSpec((tm, N_MAIN), lambda i: (i, 0)),
            pl.BlockSpec((8, tm), lambda i: (0, i)),
        ],
        out_shape=[jax.ShapeDtypeStruct((n, N_MAIN), BF16), jax.ShapeDtypeStruct((8, n), F32)],
        scratch_shapes=[pltpu.VMEM((d, TAIL_COLS), BF16), pltpu.VMEM((8, d), BF16)],
        compiler_params=_cparams("arbitrary"),
        name="inproj",
    )(h, g, w_in_b)


def _pool_kernel(a_ref, pw_ref, ps_ref, o_ref):
    s_len = a_ref.shape[0]
    row = lax.broadcasted_iota(jnp.int32, (s_len, LANES), 0)
    for gi, w in enumerate(POOL_WINDOWS):
        cols = slice(gi * LANES, (gi + 1) * LANES)
        a = a_ref[:, cols].astype(F32)
        tot = a
        sh = 1
        while sh < w:
            tot = tot + jnp.where(row >= sh, pltpu.roll(tot, sh, axis=0), 0.0)
            sh *= 2
        cnt = jnp.minimum(row + 1, w).astype(F32)
        p = tot / cnt - a
        y = jnp.dot(p.astype(BF16), pw_ref[gi].astype(BF16), preferred_element_type=F32)
        o_ref[:, cols] = (y * ps_ref[:, cols]).astype(BF16)


def _pool(proj, pool_w, pool_scale, batch, seq):
    return pl.pallas_call(
        _pool_kernel,
        grid=(batch,),
        in_specs=[
            pl.BlockSpec((seq, GROUP_WIDTH), lambda b: (b, 0)),
            _resident(pool_w.shape),
            _resident((1, GROUP_WIDTH)),
        ],
        out_specs=pl.BlockSpec((seq, GROUP_WIDTH), lambda b: (b, 0)),
        out_shape=jax.ShapeDtypeStruct((batch * seq, GROUP_WIDTH), BF16),
        compiler_params=_cparams("parallel"),
        name="pool",
    )(proj, pool_w, pool_scale)


def _head_norm(x):
    mu = jnp.mean(x, axis=-1, keepdims=True)
    d = x - mu
    var = jnp.mean(d * d, axis=-1, keepdims=True)
    return d * lax.rsqrt(var + EPS)


def _sgu_kernel(u_ref, v_ref, ng_ref, w_ref, bias_ref, o_ref, *, n_chunks):
    r_i = lax.broadcasted_iota(jnp.int32, (CHUNK, CHUNK), 0)
    c_i = lax.broadcasted_iota(jnp.int32, (CHUNK, CHUNK), 1)
    w_causal = [jnp.where(c_i <= r_i, w_ref[h], 0.0).astype(BF16) for h in range(N_HEADS)]
    for ci in range(n_chunks):
        rows = slice(ci * CHUNK, (ci + 1) * CHUNK)
        u = jax.nn.gelu(u_ref[rows, :].astype(F32))
        v = jax.nn.gelu(v_ref[rows, :].astype(F32))
        zs = []
        for h in range(N_HEADS):
            cols = slice(h * LANES, (h + 1) * LANES)
            vn = (_head_norm(v[:, cols]) * ng_ref[:, cols]).astype(BF16)
            zs.append(jnp.dot(w_causal[h], vn, preferred_element_type=F32))
        z = jnp.concatenate(zs, axis=1) + bias_ref[...]
        o_ref[rows, :] = (u * z).astype(BF16)


def _sgu(proj, norm_g, w_s, bias_tab, n_rows, rows_per_step):
    n_chunks = rows_per_step // CHUNK
    return pl.pallas_call(
        functools.partial(_sgu_kernel, n_chunks=n_chunks),
        grid=(n_rows // rows_per_step,),
        in_specs=[
            pl.BlockSpec((rows_per_step, GROUP_WIDTH), lambda i: (i, 1)),
            pl.BlockSpec((rows_per_step, GROUP_WIDTH), lambda i: (i, 2)),
            _resident((1, GROUP_WIDTH)),
            _resident(w_s.shape),
            _resident((CHUNK, GROUP_WIDTH)),
        ],
        out_specs=pl.BlockSpec((rows_per_step, GROUP_WIDTH), lambda i: (i, 0)),
        out_shape=jax.ShapeDtypeStruct((n_rows, GROUP_WIDTH), BF16),
        compiler_params=_cparams("parallel"),
        name="sgu",
    )(proj, proj, norm_g, w_s, bias_tab)


def _fox_kernel(flog_ref, bf_ref, q_ref, k_ref, v_ref, *rest, tq, tk, td, hps, scale, n_cast):
    cast_src, o_ref, cast_dst, c_scr = rest[:n_cast], rest[n_cast], rest[n_cast + 1:2 * n_cast + 1], rest[-1]
    hg = pl.program_id(1)
    qi = pl.program_id(2)
    s_len = k_ref.shape[0]
    log2e = float(np.log2(np.e))

    for src, dst in zip(cast_src, cast_dst):
        dst[...] = src[...].astype(BF16)

    @pl.when((hg == 0) & (qi == 0))
    def _():
        x = flog_ref[...] + bf_ref[:, 0:1]
        c = jnp.minimum(x, 0.0) - jnp.log1p(jnp.exp(-jnp.abs(x)))
        lane = lax.broadcasted_iota(jnp.int32, c.shape, 1)
        sh = 1
        while sh < s_len:
            c = c + jnp.where(lane >= sh, pltpu.roll(c, sh, axis=1), 0.0)
            sh *= 2
        c = c * log2e
        for j in range(s_len // td):
            c_scr[j] = c[:, j * td:(j + 1) * td]

    heads = [slice(h * LANES, (h + 1) * LANES) for h in range(hps)]
    qs = [(q_ref[:, hs].astype(F32) * (scale * log2e)).astype(BF16) for hs in heads]
    row0 = qi * tq

    def update(state, q, kb, vb, ck, diagonal):
        m, l, acc = state
        s = lax.dot_general(q, kb, (((1,), (1,)), ((), ())), preferred_element_type=F32) - ck
        if diagonal:
            n = q.shape[0]
            own = s[:, s.shape[1] - n:]
            r_i = lax.broadcasted_iota(jnp.int32, own.shape, 0)
            c_i = lax.broadcasted_iota(jnp.int32, own.shape, 1)
            own = jnp.where(c_i <= r_i, own, NEG_BIG)
            s = own if n == s.shape[1] else jnp.concatenate([s[:, :s.shape[1] - n], own], axis=1)
        m_new = jnp.maximum(m, jnp.max(s, axis=-1, keepdims=True))
        alpha = jnp.exp2(m - m_new)
        p = jnp.exp2(s - m_new)
        l = alpha * l + jnp.sum(p, axis=-1, keepdims=True)
        acc = alpha * acc + jnp.dot(p.astype(BF16), vb, preferred_element_type=F32)
        return m_new, l, acc

    def bias_row(h, key_start, width):
        first = key_start // td
        parts = [c_scr[first + i, pl.ds(hg * hps + h, 1), :] for i in range(width // td)]
        return parts[0] if len(parts) == 1 else jnp.concatenate(parts, axis=1)

    def full_block(j, carry):
        start = pl.multiple_of(j * tk, tk)
        return tuple(
            update(carry[h], qs[h], k_ref[pl.ds(start, tk), hs], v_ref[pl.ds(start, tk), hs], bias_row(h, start, tk),
                   False)
            for h, hs in enumerate(heads))

    carry = tuple((jnp.full((tq, 1), NEG_BIG, F32), jnp.zeros((tq, 1), F32), jnp.zeros((tq, LANES), F32))
                  for _ in heads)
    carry = lax.fori_loop(0, row0 // tk, full_block, carry)

    start = pl.multiple_of(row0, tq)
    for h, hs in enumerate(heads):
        m, l, acc = carry[h]
        for r in range(tq // td):
            rows = slice(r * td, (r + 1) * td)
            n_keys = (r + 1) * td
            _, l_r, acc_r = update((m[rows], l[rows], acc[rows]), qs[h][rows], k_ref[pl.ds(start, n_keys), hs],
                                   v_ref[pl.ds(start, n_keys), hs], bias_row(h, start, n_keys), True)
            o_ref[rows, hs] = (acc_r / l_r).astype(BF16)


def _fox(proj, flog_t, bf_tab, batch, seq, tq, tk, td, hps, cast_weights, layer):
    assert tq % td == 0 and tk % td == 0 and tq % tk == 0
    nq = seq // tq
    width = hps * LANES
    per_piece = GROUP_WIDTH // width
    n_steps = batch * per_piece * nq

    def step(b, g, i):
        return (b * per_piece + g) * nq + i

    cast_in, cast_out, cast_shapes = [], [], []
    for w in cast_weights:
        _, rows, cols = w.shape
        slab = rows // n_steps
        assert slab * n_steps == rows and slab % 16 == 0, (w.shape, n_steps)
        cast_in.append(pl.BlockSpec((None, slab, cols), lambda b, g, i: (layer, step(b, g, i), 0)))
        cast_out.append(pl.BlockSpec((slab, cols), lambda b, g, i: (step(b, g, i), 0)))
        cast_shapes.append(jax.ShapeDtypeStruct((rows, cols), BF16))

    outs = pl.pallas_call(
        functools.partial(_fox_kernel, tq=tq, tk=tk, td=td, hps=hps, scale=LANES ** -0.5, n_cast=len(cast_weights)),
        grid=(batch, per_piece, nq),
        in_specs=[
            pl.BlockSpec((8, seq), lambda b, g, i: (0, b)),
            _resident((8, LANES)),
            pl.BlockSpec((tq, width), lambda b, g, i: (b * nq + i, 3 * per_piece + g)),
            pl.BlockSpec((seq, width), lambda b, g, i: (b, 4 * per_piece + g)),
            pl.BlockSpec((seq, width), lambda b, g, i: (b, 5 * per_piece + g)),
        ] + cast_in,
        out_specs=[pl.BlockSpec((tq, width), lambda b, g, i: (b * nq + i, g))] + cast_out,
        out_shape=[jax.ShapeDtypeStruct((batch * seq, GROUP_WIDTH), BF16)] + cast_shapes,
        scratch_shapes=[pltpu.VMEM((seq // td, 8, td), F32)],
        compiler_params=_cparams("parallel", "arbitrary", "arbitrary"),
        name="fox",
    )(flog_t, bf_tab, proj, proj, proj, *cast_weights)
    return outs[0], outs[1:]


def _ret_kernel(q_ref, k_ref, v_ref, g_ref, cos_ref, sin_ref, hm_ref, zeta_ref, decay_ref, xi_ref,
                cd_ref, bd_ref, ng_ref, o_ref, state_scr, *, n_chunks):
    @pl.when(pl.program_id(1) == 0)
    def _():
        state_scr[...] = jnp.zeros_like(state_scr)

    def rope(x, cos, sin):
        x1, x2 = x[:, :LANES], x[:, LANES:]
        return jnp.concatenate([x1 * cos - x2 * sin, x1 * sin + x2 * cos], axis=1)

    for ci in range(n_chunks):
        rows = slice(ci * CHUNK, (ci + 1) * CHUNK)
        cos, sin = cos_ref[rows, :], sin_ref[rows, :]
        qr = rope(q_ref[rows, :].astype(F32), cos, sin)
        kr = rope(k_ref[rows, :].astype(F32), cos, sin) * (RET_QK ** -0.5)
        v = v_ref[rows, :]
        qb = qr.astype(BF16)
        k_heads = jnp.concatenate([(kr * hm_ref[h:h + 1, :]).astype(BF16) for h in range(N_HEADS)], axis=0)
        s = lax.dot_general(qb, k_heads, (((1,), (1,)), ((), ())), preferred_element_type=F32)
        sb = (s * decay_ref[...]).astype(BF16)
        y_intra = jnp.concatenate(
            [jnp.dot(sb[:, h * LANES:(h + 1) * LANES], v[:, h * LANES:(h + 1) * LANES], preferred_element_type=F32)
             for h in range(N_HEADS)], axis=1)
        state = state_scr[...]
        y_cross = jnp.dot(qb, state.astype(BF16), preferred_element_type=F32) * xi_ref[...]
        kz_t = jnp.transpose(kr * zeta_ref[...]).astype(BF16)
        upd = jnp.dot(kz_t, v, preferred_element_type=F32)
        state_scr[...] = state * cd_ref[...] + upd * bd_ref[...]
        y = y_intra + y_cross
        yn = jnp.concatenate([_head_norm(y[:, h * LANES:(h + 1) * LANES]) for h in range(N_HEADS)], axis=1)
        g = g_ref[rows, :].astype(F32)
        o_ref[rows, :] = (jax.nn.silu(g) * (yn * ng_ref[...])).astype(BF16)


def _ret_tables(seq):
    half = RET_QK // 2
    inv = np.exp(-(np.arange(half, dtype=np.float32) / half) * np.float32(np.log(ROPE_BASE))).astype(np.float32)
    ang = np.arange(seq, dtype=np.float32)[:, None] * inv[None, :]
    cos = np.tile(np.cos(ang).astype(np.float32), (1, N_HEADS))
    sin = np.tile(np.sin(ang).astype(np.float32), (1, N_HEADS))
    gamma = (1.0 - np.exp((-5.0 - np.arange(N_HEADS, dtype=np.float32)) * np.float32(np.log(2.0)))).astype(np.float32)
    log_gamma = np.log(gamma).astype(np.float32)
    l = np.arange(CHUNK, dtype=np.float32)
    diff = l[:, None] - l[None, :]
    decay = np.where(diff[None] >= 0, np.exp(np.maximum(diff, 0.0)[None] * log_gamma[:, None, None]), 0.0)
    decay = np.transpose(decay, (1, 0, 2)).reshape(CHUNK, N_HEADS * CHUNK).astype(np.float32)
    xi = np.repeat(np.exp((l + 1.0)[:, None] * log_gamma[None, :]), LANES, axis=1).astype(np.float32)
    lane_head = np.tile(np.repeat(np.arange(N_HEADS), half), 2)
    zeta = np.exp((CHUNK - 1.0 - l)[:, None] * log_gamma[lane_head][None, :]).astype(np.float32)
    hm = np.zeros((8, 2 * LANES), np.float32)
    for h in range(N_HEADS):
        hm[h] = lane_head == h
    col_head = np.repeat(np.arange(N_HEADS), LANES)
    cd = np.exp(CHUNK * log_gamma)[col_head][None, :].astype(np.float32)
    bd = (lane_head[:, None] == col_head[None, :]).astype(np.float32)
    return tuple(jnp.asarray(t) for t in (cos, sin, hm, zeta, decay, xi, cd, bd))


def _ret(proj, norm_g, tables, batch, seq, rows_per_step):
    cos, sin, hm, zeta, decay, xi, cd, bd = tables
    n_chunks = rows_per_step // CHUNK
    steps = seq // rows_per_step
    half_blk = rows_per_step, 2 * LANES
    full_blk = rows_per_step, GROUP_WIDTH
    return pl.pallas_call(
        functools.partial(_ret_kernel, n_chunks=n_chunks),
        grid=(batch, steps),
        in_specs=[
            pl.BlockSpec(half_blk, lambda b, i: (b * steps + i, 12)),
            pl.BlockSpec(half_blk, lambda b, i: (b * steps + i, 13)),
            pl.BlockSpec(full_blk, lambda b, i: (b * steps + i, 7)),
            pl.BlockSpec(full_blk, lambda b, i: (b * steps + i, 8)),
            pl.BlockSpec((rows_per_step, LANES), lambda b, i: (i, 0)),
            pl.BlockSpec((rows_per_step, LANES), lambda b, i: (i, 0)),
            _resident(hm.shape), _resident(zeta.shape), _resident(decay.shape), _resident(xi.shape),
            _resident(cd.shape), _resident(bd.shape), _resident((1, GROUP_WIDTH)),
        ],
        out_specs=pl.BlockSpec(full_blk, lambda b, i: (b * steps + i, 0)),
        out_shape=jax.ShapeDtypeStruct((batch * seq, GROUP_WIDTH), BF16),
        scratch_shapes=[pltpu.VMEM((2 * LANES, GROUP_WIDTH), F32)],
        compiler_params=_cparams("parallel", "arbitrary"),
        name="retention",
    )(proj, proj, proj, proj, cos, sin, hm, zeta, decay, xi, cd, bd, norm_g)


def _outproj_kernel(h_ref, ya_ref, yb_ref, yc_ref, yd_ref, w_ref, o_ref, mix_scr):
    for gi, y_ref in enumerate((ya_ref, yb_ref, yc_ref, yd_ref)):
        mix_scr[:, gi * GROUP_WIDTH:(gi + 1) * GROUP_WIDTH] = y_ref[...]
    mix = mix_scr[...]
    for c in range(o_ref.shape[1] // GROUP_WIDTH):
        cols = slice(c * GROUP_WIDTH, (c + 1) * GROUP_WIDTH)
        o_ref[:, cols] = h_ref[:, cols] + jnp.dot(mix, w_ref[:, cols], preferred_element_type=F32)


def _outproj(h, ys, w_out, tm):
    n, d = h.shape
    y_spec = pl.BlockSpec((tm, GROUP_WIDTH), lambda i: (i, 0))
    w_spec = _resident(w_out.shape)
    return pl.pallas_call(
        _outproj_kernel,
        grid=(n // tm,),
        in_specs=[pl.BlockSpec((tm, d), lambda i: (i, 0)), y_spec, y_spec, y_spec, y_spec, w_spec],
        out_specs=pl.BlockSpec((tm, d), lambda i: (i, 0)),
        out_shape=jax.ShapeDtypeStruct((n, d), F32),
        scratch_shapes=[pltpu.VMEM((tm, N_HEADS * GROUP_WIDTH), BF16)],
        compiler_params=_cparams("parallel"),
        name="outproj",
    )(h, *ys, w_out)


def _mlp_kernel(h_ref, g_ref, w1_ref, w2_ref, gf_ref, o_ref, hn_scr, *, final_norm, n_split):
    f = pl.program_id(1)
    sub = h_ref.shape[0] // n_split
    row_groups = [slice(r * sub, (r + 1) * sub) for r in range(n_split)]

    @pl.when(f == 0)
    def _():
        for rows in row_groups:
            x = h_ref[rows, :]
            hn_scr[rows, :] = (x * _rms_scale(x) * g_ref[...]).astype(BF16)
            o_ref[rows, :] = x

    for rows in row_groups:
        a = jnp.dot(hn_scr[rows, :], w1_ref[...], preferred_element_type=F32)
        a = jnp.square(jnp.maximum(a, 0.0)).astype(BF16)
        o_ref[rows, :] += jnp.dot(a, w2_ref[...], preferred_element_type=F32)

    if final_norm:
        @pl.when(f == pl.num_programs(1) - 1)
        def _():
            for rows in row_groups:
                y = o_ref[rows, :]
                o_ref[rows, :] = y * _rms_scale(y) * gf_ref[...]


def _mlp(h, g, w1, w2, g_final, tm, tf, final_norm):
    n, d = h.shape
    d_ff = w1.shape[1]
    return pl.pallas_call(
        functools.partial(_mlp_kernel, final_norm=final_norm, n_split=max(1, tm // 512)),
        grid=(n // tm, d_ff // tf),
        in_specs=[
            pl.BlockSpec((tm, d), lambda i, f: (i, 0)),
            _resident((1, d)),
            pl.BlockSpec((d, tf), lambda i, f: (0, f)),
            pl.BlockSpec((tf, d), lambda i, f: (f, 0)),
            _resident((1, d)),
        ],
        out_specs=pl.BlockSpec((tm, d), lambda i, f: (i, 0)),
        out_shape=jax.ShapeDtypeStruct((n, d), F32),
        scratch_shapes=[pltpu.VMEM((tm, d), BF16)],
        compiler_params=_cparams("parallel", "arbitrary"),
        name="mlp",
    )(h, g, w1, w2, g_final)


def kernel(x, norm_mix_g, w_in, fox_b_f, pool_w, pool_scale, sgu_norm_g, sgu_w_s, sgu_b, ret_norm_g, w_out,
           norm_mlp_g, w_ff1, w_ff2, norm_final_g):
    batch, seq, d = x.shape
    depth = w_in.shape[0]
    n = batch * seq
    tm = min(512, n)
    tm_mlp = min(1024, n)
    rows_per_step = min(512, seq)
    fox_tiles = (min(FOX_TQ, seq), min(FOX_TK, seq), min(FOX_TD, seq), FOX_HEADS_PER_STEP)
    tables = _ret_tables(seq)

    w_in_b = w_in.astype(BF16)

    h = x.reshape(n, d)
    g_final = norm_final_g.reshape(1, d)
    for layer in range(depth):
        proj, flog_t = _inproj(h, norm_mix_g[layer].reshape(1, d), w_in_b, layer, tm)

        y_a = _pool(proj, pool_w[layer], pool_scale[layer].reshape(1, GROUP_WIDTH), batch, seq)
        sgu_bias = jnp.repeat(jnp.transpose(sgu_b[layer]), LANES, axis=1)
        y_b = _sgu(proj, sgu_norm_g[layer].reshape(1, GROUP_WIDTH), sgu_w_s[layer], sgu_bias, n, rows_per_step)
        bf_tab = jnp.zeros((8, LANES), F32).at[:N_HEADS].set(jnp.broadcast_to(fox_b_f[layer][:, None], (N_HEADS, LANES)))
        y_c, (w_out_b, w_ff1_b, w_ff2_b) = _fox(proj, flog_t, bf_tab, batch, seq, *fox_tiles,
                                                cast_weights=(w_out, w_ff1, w_ff2), layer=layer)
        y_d = _ret(proj, ret_norm_g[layer].reshape(1, GROUP_WIDTH), tables, batch, seq, rows_per_step)

        h = _outproj(h, (y_a, y_b, y_c, y_d), w_out_b, tm)
        h = _mlp(h, norm_mlp_g[layer].reshape(1, d), w_ff1_b, w_ff2_b, g_final, tm_mlp, 512,
                 final_norm=(layer == depth - 1))
    return h.reshape(batch, seq, d)
```

```python
import functools

import numpy as np
import jax
import jax.numpy as jnp
from jax import lax
from jax.experimental import pallas as pl
from jax.experimental.pallas import tpu as pltpu

F32 = jnp.float32
BF16 = jnp.bfloat16

EPS = 1e-6
NEG_BIG = -1e30
GROUP_WIDTH = 512
LANES = 128
N_HEADS = 4
POOL_WINDOWS = (2, 4, 8, 16)
CHUNK = 128
CHUNKS_PER_ITER = 4
RET_QK = 64
ROPE_BASE = 10000.0
N_MAIN = 9 * GROUP_WIDTH
F_COL0 = 6 * GROUP_WIDTH
TAIL_COLS = N_MAIN - F_COL0
TAIL_WINDOW = -(-(N_HEADS + TAIL_COLS) // LANES) * LANES
V7X_VMEM_LIMIT = 56 * 1024 * 1024
FOX_TQ, FOX_TK, FOX_TD, FOX_HEADS_PER_STEP = 512, 512, 512, 4


def _cparams(*sem):
    return pltpu.CompilerParams(dimension_semantics=sem, vmem_limit_bytes=V7X_VMEM_LIMIT)


def _resident(shape):
    nd = len(shape)
    return pl.BlockSpec(shape, lambda *_: (0,) * nd, pipeline_mode=pl.Buffered(1))


def _rms_scale(x):
    return lax.rsqrt(jnp.mean(x * x, axis=-1, keepdims=True) + EPS)


def _for_each_chunk(n_chunks, body):
    per_iter = min(CHUNKS_PER_ITER, n_chunks)
    assert n_chunks % per_iter == 0

    def group(gi, _):
        for ci in range(per_iter):
            body(pl.ds(pl.multiple_of((gi * per_iter + ci) * CHUNK, CHUNK), CHUNK))
        return 0

    lax.fori_loop(0, n_chunks // per_iter, group, 0)


def _retention_slab_permutation():
    n = 2 * N_HEADS * RET_QK
    src_row = lax.broadcasted_iota(jnp.int32, (n, n), 0)
    j = lax.broadcasted_iota(jnp.int32, (n, n), 1)
    part, jj = j >> 8, j & 255
    half, head, freq = jj >> 7, (jj & 127) >> 5, jj & 31
    src = (part << 8) + (head << 6) + (half << 5) + freq
    return jnp.where(src_row == src, 1.0, 0.0).astype(BF16)


def _inproj_kernel(h_ref, g_ref, w_ref, proj_ref, flog_ref, w_tail_scr, wf_scr):
    d = w_ref.shape[0]

    @pl.when(pl.program_id(0) == 0)
    def _():
        perm = _retention_slab_permutation()
        rb = 256
        for r in range(d // rb):
            rows = slice(r * rb, (r + 1) * rb)
            win = w_ref[rows, F_COL0:F_COL0 + TAIL_WINDOW].astype(F32)
            tail = pltpu.roll(win, TAIL_WINDOW - N_HEADS, axis=1)[:, :TAIL_COLS].astype(BF16)
            qk = jnp.dot(tail[:, :GROUP_WIDTH], perm, preferred_element_type=F32)
            w_tail_scr[rows, :GROUP_WIDTH] = qk.astype(BF16)
            w_tail_scr[rows, GROUP_WIDTH:] = tail[:, GROUP_WIDTH:]
        wf_t = jnp.transpose(w_ref[:, F_COL0:F_COL0 + LANES].astype(F32))[:8, :]
        keep = lax.broadcasted_iota(jnp.int32, wf_t.shape, 0) < N_HEADS
        wf_scr[...] = jnp.where(keep, wf_t, 0.0).astype(BF16)

    x = h_ref[...]
    xn = (x * _rms_scale(x) * g_ref[...]).astype(BF16)
    for c in range(N_MAIN // GROUP_WIDTH):
        cols = slice(c * GROUP_WIDTH, (c + 1) * GROUP_WIDTH)
        w = w_ref[:, cols] if c < F_COL0 // GROUP_WIDTH else w_tail_scr[:, c * GROUP_WIDTH - F_COL0:(c + 1) * GROUP_WIDTH - F_COL0]
        proj_ref[:, cols] = jnp.dot(xn, w, preferred_element_type=F32).astype(BF16)
    flog_ref[...] = lax.dot_general(wf_scr[...], xn, (((1,), (1,)), ((), ())), preferred_element_type=F32)


def _inproj(h, g, w_in_b, layer, tm):
    n, d = h.shape
    w_spec = pl.BlockSpec((None,) + w_in_b.shape[1:], lambda i: (layer, 0, 0), pipeline_mode=pl.Buffered(1))
    return pl.pallas_call(
        _inproj_kernel,
        grid=(n // tm,),
        in_specs=[pl.BlockSpec((tm, d), lambda i: (i, 0)), _resident((1, d)), w_spec],
        out_specs=[
            pl.BlockSpec((tm, N_MAIN), lambda i: (i, 0)),
            pl.BlockSpec((8, tm), lambda i: (0, i)),
        ],
        out_shape=[jax.ShapeDtypeStruct((n, N_MAIN), BF16), jax.ShapeDtypeStruct((8, n), F32)],
        scratch_shapes=[pltpu.VMEM((d, TAIL_COLS), BF16), pltpu.VMEM((8, d), BF16)],
        compiler_params=_cparams("arbitrary"),
        name="inproj",
    )(h, g, w_in_b)


def _pool_kernel(a_ref, pw_ref, ps_ref, o_ref):
    s_len = a_ref.shape[0]
    row = lax.broadcasted_iota(jnp.int32, (s_len, LANES), 0)
    for gi, w in enumerate(POOL_WINDOWS):
        cols = slice(gi * LANES, (gi + 1) * LANES)
        a = a_ref[:, cols].astype(F32)
        tot = a
        sh = 1
        while sh < w:
            tot = tot + jnp.where(row >= sh, pltpu.roll(tot, sh, axis=0), 0.0)
            sh *= 2
        cnt = jnp.minimum(row + 1, w).astype(F32)
        p = tot / cnt - a
        y = jnp.dot(p.astype(BF16), pw_ref[gi].astype(BF16), preferred_element_type=F32)
        o_ref[:, cols] = (y * ps_ref[:, cols]).astype(BF16)


def _pool(proj, pool_w, pool_scale, batch, seq):
    return pl.pallas_call(
        _pool_kernel,
        grid=(batch,),
        in_specs=[
            pl.BlockSpec((seq, GROUP_WIDTH), lambda b: (b, 0)),
            _resident(pool_w.shape),
            _resident((1, GROUP_WIDTH)),
        ],
        out_specs=pl.BlockSpec((seq, GROUP_WIDTH), lambda b: (b, 0)),
        out_shape=jax.ShapeDtypeStruct((batch * seq, GROUP_WIDTH), BF16),
        compiler_params=_cparams("parallel"),
        name="pool",
    )(proj, pool_w, pool_scale)


def _head_norm(x):
    mu = jnp.mean(x, axis=-1, keepdims=True)
    d = x - mu
    var = jnp.mean(d * d, axis=-1, keepdims=True)
    return d * lax.rsqrt(var + EPS)


def _sgu_kernel(u_ref, v_ref, ng_ref, w_ref, bias_ref, o_ref, *, n_chunks):
    r_i = lax.broadcasted_iota(jnp.int32, (CHUNK, CHUNK), 0)
    c_i = lax.broadcasted_iota(jnp.int32, (CHUNK, CHUNK), 1)
    w_causal = [jnp.where(c_i <= r_i, w_ref[h], 0.0).astype(BF16) for h in range(N_HEADS)]

    def chunk(rows):
        u = jax.nn.gelu(u_ref[rows, :].astype(F32))
        v = jax.nn.gelu(v_ref[rows, :].astype(F32))
        zs = []
        for h in range(N_HEADS):
            cols = slice(h * LANES, (h + 1) * LANES)
            vn = (_head_norm(v[:, cols]) * ng_ref[:, cols]).astype(BF16)
            zs.append(jnp.dot(w_causal[h], vn, preferred_element_type=F32))
        z = jnp.concatenate(zs, axis=1) + bias_ref[...]
        o_ref[rows, :] = (u * z).astype(BF16)

    _for_each_chunk(n_chunks, chunk)


def _sgu(proj, norm_g, w_s, bias_tab, n_rows, rows_per_step):
    n_chunks = rows_per_step // CHUNK
    return pl.pallas_call(
        functools.partial(_sgu_kernel, n_chunks=n_chunks),
        grid=(n_rows // rows_per_step,),
        in_specs=[
            pl.BlockSpec((rows_per_step, GROUP_WIDTH), lambda i: (i, 1)),
            pl.BlockSpec((rows_per_step, GROUP_WIDTH), lambda i: (i, 2)),
            _resident((1, GROUP_WIDTH)),
            _resident(w_s.shape),
            _resident((CHUNK, GROUP_WIDTH)),
        ],
        out_specs=pl.BlockSpec((rows_per_step, GROUP_WIDTH), lambda i: (i, 0)),
        out_shape=jax.ShapeDtypeStruct((n_rows, GROUP_WIDTH), BF16),
        compiler_params=_cparams("parallel"),
        name="sgu",
    )(proj, proj, norm_g, w_s, bias_tab)


def _fox_kernel(flog_ref, bf_ref, q_ref, k_ref, v_ref, *rest, tq, tk, td, hps, scale, n_cast):
    cast_src, o_ref, cast_dst, c_scr = rest[:n_cast], rest[n_cast], rest[n_cast + 1:2 * n_cast + 1], rest[-1]
    hg = pl.program_id(1)
    qi = pl.program_id(2)
    s_len = k_ref.shape[0]
    log2e = float(np.log2(np.e))

    for src, dst in zip(cast_src, cast_dst):
        dst[...] = src[...].astype(BF16)

    @pl.when((hg == 0) & (qi == 0))
    def _():
        x = flog_ref[...] + bf_ref[:, 0:1]
        c = jnp.minimum(x, 0.0) - jnp.log1p(jnp.exp(-jnp.abs(x)))
        lane = lax.broadcasted_iota(jnp.int32, c.shape, 1)
        sh = 1
        while sh < s_len:
            c = c + jnp.where(lane >= sh, pltpu.roll(c, sh, axis=1), 0.0)
            sh *= 2
        c = c * log2e
        for j in range(s_len // td):
            c_scr[j] = c[:, j * td:(j + 1) * td]

    heads = [slice(h * LANES, (h + 1) * LANES) for h in range(hps)]
    qs = [(q_ref[:, hs].astype(F32) * (scale * log2e)).astype(BF16) for hs in heads]
    row0 = qi * tq

    def update(state, q, kb, vb, ck, diagonal):
        m, l, acc = state
        s = lax.dot_general(q, kb, (((1,), (1,)), ((), ())), preferred_element_type=F32) - ck
        if diagonal:
            n = q.shape[0]
            own = s[:, s.shape[1] - n:]
            r_i = lax.broadcasted_iota(jnp.int32, own.shape, 0)
            c_i = lax.broadcasted_iota(jnp.int32, own.shape, 1)
            own = jnp.where(c_i <= r_i, own, NEG_BIG)
            s = own if n == s.shape[1] else jnp.concatenate([s[:, :s.shape[1] - n], own], axis=1)
        m_new = jnp.maximum(m, jnp.max(s, axis=-1, keepdims=True))
        alpha = jnp.exp2(m - m_new)
        p = jnp.exp2(s - m_new)
        l = alpha * l + jnp.sum(p, axis=-1, keepdims=True)
        acc = alpha * acc + jnp.dot(p.astype(BF16), vb, preferred_element_type=F32)
        return m_new, l, acc

    def bias_row(h, key_start, width):
        first = key_start // td
        parts = [c_scr[first + i, pl.ds(hg * hps + h, 1), :] for i in range(width // td)]
        return parts[0] if len(parts) == 1 else jnp.concatenate(parts, axis=1)

    def full_block(j, carry):
        start = pl.multiple_of(j * tk, tk)
        return tuple(
            update(carry[h], qs[h], k_ref[pl.ds(start, tk), hs], v_ref[pl.ds(start, tk), hs], bias_row(h, start, tk),
                   False)
            for h, hs in enumerate(heads))

    carry = tuple((jnp.full((tq, 1), NEG_BIG, F32), jnp.zeros((tq, 1), F32), jnp.zeros((tq, LANES), F32))
                  for _ in heads)
    carry = lax.fori_loop(0, row0 // tk, full_block, carry)

    start = pl.multiple_of(row0, tq)
    for h, hs in enumerate(heads):
        m, l, acc = carry[h]
        for r in range(tq // td):
            rows = slice(r * td, (r + 1) * td)
            n_keys = (r + 1) * td
            _, l_r, acc_r = update((m[rows], l[rows], acc[rows]), qs[h][rows], k_ref[pl.ds(start, n_keys), hs],
                                   v_ref[pl.ds(start, n_keys), hs], bias_row(h, start, n_keys), True)
            o_ref[rows, hs] = (acc_r / l_r).astype(BF16)


def _fox(proj, flog_t, bf_tab, batch, seq, tq, tk, td, hps, cast_weights, layer):
    assert tq % td == 0 and tk % td == 0 and tq % tk == 0
    nq = seq // tq
    width = hps * LANES
    per_piece = GROUP_WIDTH // width
    n_steps = batch * per_piece * nq

    def step(b, g, i):
        return (b * per_piece + g) * nq + i

    cast_in, cast_out, cast_shapes = [], [], []
    for w in cast_weights:
        _, rows, cols = w.shape
        slab = rows // n_steps
        assert slab * n_steps == rows and slab % 16 == 0, (w.shape, n_steps)
        cast_in.append(pl.BlockSpec((None, slab, cols), lambda b, g, i: (layer, step(b, g, i), 0)))
        cast_out.append(pl.BlockSpec((slab, cols), lambda b, g, i: (step(b, g, i), 0)))
        cast_shapes.append(jax.ShapeDtypeStruct((rows, cols), BF16))

    outs = pl.pallas_call(
        functools.partial(_fox_kernel, tq=tq, tk=tk, td=td, hps=hps, scale=LANES ** -0.5, n_cast=len(cast_weights)),
        grid=(batch, per_piece, nq),
        in_specs=[
            pl.BlockSpec((8, seq), lambda b, g, i: (0, b)),
            _resident((8, LANES)),
            pl.BlockSpec((tq, width), lambda b, g, i: (b * nq + i, 3 * per_piece + g)),
            pl.BlockSpec((seq, width), lambda b, g, i: (b, 4 * per_piece + g)),
            pl.BlockSpec((seq, width), lambda b, g, i: (b, 5 * per_piece + g)),
        ] + cast_in,
        out_specs=[pl.BlockSpec((tq, width), lambda b, g, i: (b * nq + i, g))] + cast_out,
        out_shape=[jax.ShapeDtypeStruct((batch * seq, GROUP_WIDTH), BF16)] + cast_shapes,
        scratch_shapes=[pltpu.VMEM((seq // td, 8, td), F32)],
        compiler_params=_cparams("parallel", "arbitrary", "arbitrary"),
        name="fox",
    )(flog_t, bf_tab, proj, proj, proj, *cast_weights)
    return outs[0], outs[1:]


def _ret_kernel(q_ref, k_ref, v_ref, g_ref, cos_ref, sin_ref, hm_ref, zeta_ref, decay_ref, xi_ref,
                cd_ref, bd_ref, ng_ref, o_ref, state_scr, *, n_chunks):
    @pl.when(pl.program_id(1) == 0)
    def _():
        state_scr[...] = jnp.zeros_like(state_scr)

    def rope(x, cos, sin):
        x1, x2 = x[:, :LANES], x[:, LANES:]
        return jnp.concatenate([x1 * cos - x2 * sin, x1 * sin + x2 * cos], axis=1)

    def chunk(rows):
        cos, sin = cos_ref[rows, :], sin_ref[rows, :]
        qr = rope(q_ref[rows, :].astype(F32), cos, sin)
        kr = rope(k_ref[rows, :].astype(F32), cos, sin) * (RET_QK ** -0.5)
        v = v_ref[rows, :]
        qb = qr.astype(BF16)
        k_heads = jnp.concatenate([(kr * hm_ref[h:h + 1, :]).astype(BF16) for h in range(N_HEADS)], axis=0)
        s = lax.dot_general(qb, k_heads, (((1,), (1,)), ((), ())), preferred_element_type=F32)
        sb = (s * decay_ref[...]).astype(BF16)
        y_intra = jnp.concatenate(
            [jnp.dot(sb[:, h * LANES:(h + 1) * LANES], v[:, h * LANES:(h + 1) * LANES], preferred_element_type=F32)
             for h in range(N_HEADS)], axis=1)
        state = state_scr[...]
        y_cross = jnp.dot(qb, state.astype(BF16), preferred_element_type=F32) * xi_ref[...]
        kz_t = jnp.transpose(kr * zeta_ref[...]).astype(BF16)
        upd = jnp.dot(kz_t, v, preferred_element_type=F32)
        state_scr[...] = state * cd_ref[...] + upd * bd_ref[...]
        y = y_intra + y_cross
        yn = jnp.concatenate([_head_norm(y[:, h * LANES:(h + 1) * LANES]) for h in range(N_HEADS)], axis=1)
        g = g_ref[rows, :].astype(F32)
        o_ref[rows, :] = (jax.nn.silu(g) * (yn * ng_ref[...])).astype(BF16)

    _for_each_chunk(n_chunks, chunk)


def _ret_tables(seq):
    half = RET_QK // 2
    inv = np.exp(-(np.arange(half, dtype=np.float32) / half) * np.float32(np.log(ROPE_BASE))).astype(np.float32)
    ang = np.arange(seq, dtype=np.float32)[:, None] * inv[None, :]
    cos = np.tile(np.cos(ang).astype(np.float32), (1, N_HEADS))
    sin = np.tile(np.sin(ang).astype(np.float32), (1, N_HEADS))
    gamma = (1.0 - np.exp((-5.0 - np.arange(N_HEADS, dtype=np.float32)) * np.float32(np.log(2.0)))).astype(np.float32)
    log_gamma = np.log(gamma).astype(np.float32)
    l = np.arange(CHUNK, dtype=np.float32)
    diff = l[:, None] - l[None, :]
    decay = np.where(diff[None] >= 0, np.exp(np.maximum(diff, 0.0)[None] * log_gamma[:, None, None]), 0.0)
    decay = np.transpose(decay, (1, 0, 2)).reshape(CHUNK, N_HEADS * CHUNK).astype(np.float32)
    xi = np.repeat(np.exp((l + 1.0)[:, None] * log_gamma[None, :]), LANES, axis=1).astype(np.float32)
    lane_head = np.tile(np.repeat(np.arange(N_HEADS), half), 2)
    zeta = np.exp((CHUNK - 1.0 - l)[:, None] * log_gamma[lane_head][None, :]).astype(np.float32)
    hm = np.zeros((8, 2 * LANES), np.float32)
    for h in range(N_HEADS):
        hm[h] = lane_head == h
    col_head = np.repeat(np.arange(N_HEADS), LANES)
    cd = np.exp(CHUNK * log_gamma)[col_head][None, :].astype(np.float32)
    bd = (lane_head[:, None] == col_head[None, :]).astype(np.float32)
    return tuple(jnp.asarray(t) for t in (cos, sin, hm, zeta, decay, xi, cd, bd))


def _ret(proj, norm_g, tables, batch, seq, rows_per_step):
    cos, sin, hm, zeta, decay, xi, cd, bd = tables
    n_chunks = rows_per_step // CHUNK
    steps = seq // rows_per_step
    half_blk = rows_per_step, 2 * LANES
    full_blk = rows_per_step, GROUP_WIDTH
    return pl.pallas_call(
        functools.partial(_ret_kernel, n_chunks=n_chunks),
        grid=(batch, steps),
        in_specs=[
            pl.BlockSpec(half_blk, lambda b, i: (b * steps + i, 12)),
            pl.BlockSpec(half_blk, lambda b, i: (b * steps + i, 13)),
            pl.BlockSpec(full_blk, lambda b, i: (b * steps + i, 7)),
            pl.BlockSpec(full_blk, lambda b, i: (b * steps + i, 8)),
            pl.BlockSpec((rows_per_step, LANES), lambda b, i: (i, 0)),
            pl.BlockSpec((rows_per_step, LANES), lambda b, i: (i, 0)),
            _resident(hm.shape), _resident(zeta.shape), _resident(decay.shape), _resident(xi.shape),
            _resident(cd.shape), _resident(bd.shape), _resident((1, GROUP_WIDTH)),
        ],
        out_specs=pl.BlockSpec(full_blk, lambda b, i: (b * steps + i, 0)),
        out_shape=jax.ShapeDtypeStruct((batch * seq, GROUP_WIDTH), BF16),
        scratch_shapes=[pltpu.VMEM((2 * LANES, GROUP_WIDTH), F32)],
        compiler_params=_cparams("parallel", "arbitrary"),
        name="retention",
    )(proj, proj, proj, proj, cos, sin, hm, zeta, decay, xi, cd, bd, norm_g)


def _outproj_kernel(h_ref, ya_ref, yb_ref, yc_ref, yd_ref, w_ref, o_ref, mix_scr):
    for gi, y_ref in enumerate((ya_ref, yb_ref, yc_ref, yd_ref)):
        mix_scr[:, gi * GROUP_WIDTH:(gi + 1) * GROUP_WIDTH] = y_ref[...]
    mix = mix_scr[...]
    for c in range(o_ref.shape[1] // GROUP_WIDTH):
        cols = slice(c * GROUP_WIDTH, (c + 1) * GROUP_WIDTH)
        o_ref[:, cols] = h_ref[:, cols] + jnp.dot(mix, w_ref[:, cols], preferred_element_type=F32)


def _outproj(h, ys, w_out, tm):
    n, d = h.shape
    y_spec = pl.BlockSpec((tm, GROUP_WIDTH), lambda i: (i, 0))
    w_spec = _resident(w_out.shape)
    return pl.pallas_call(
        _outproj_kernel,
        grid=(n // tm,),
        in_specs=[pl.BlockSpec((tm, d), lambda i: (i, 0)), y_spec, y_spec, y_spec, y_spec, w_spec],
        out_specs=pl.BlockSpec((tm, d), lambda i: (i, 0)),
        out_shape=jax.ShapeDtypeStruct((n, d), F32),
        scratch_shapes=[pltpu.VMEM((tm, N_HEADS * GROUP_WIDTH), BF16)],
        compiler_params=_cparams("parallel"),
        name="outproj",
    )(h, *ys, w_out)


def _mlp_kernel(h_ref, g_ref, w1_ref, w2_ref, gf_ref, o_ref, hn_scr, *, final_norm, n_split):
    f = pl.program_id(1)
    sub = h_ref.shape[0] // n_split
    row_groups = [slice(r * sub, (r + 1) * sub) for r in range(n_split)]

    @pl.when(f == 0)
    def _():
        for rows in row_groups:
            x = h_ref[rows, :]
            hn_scr[rows, :] = (x * _rms_scale(x) * g_ref[...]).astype(BF16)
            o_ref[rows, :] = x

    for rows in row_groups:
        a = jnp.dot(hn_scr[rows, :], w1_ref[...], preferred_element_type=F32)
        a = jnp.square(jnp.maximum(a, 0.0)).astype(BF16)
        o_ref[rows, :] += jnp.dot(a, w2_ref[...], preferred_element_type=F32)

    if final_norm:
        @pl.when(f == pl.num_programs(1) - 1)
        def _():
            for rows in row_groups:
                y = o_ref[rows, :]
                o_ref[rows, :] = y * _rms_scale(y) * gf_ref[...]


def _mlp(h, g, w1, w2, g_final, tm, tf, final_norm):
    n, d = h.shape
    d_ff = w1.shape[1]
    return pl.pallas_call(
        functools.partial(_mlp_kernel, final_norm=final_norm, n_split=max(1, tm // 512)),
        grid=(n // tm, d_ff // tf),
        in_specs=[
            pl.BlockSpec((tm, d), lambda i, f: (i, 0)),
            _resident((1, d)),
            pl.BlockSpec((d, tf), lambda i, f: (0, f)),
            pl.BlockSpec((tf, d), lambda i, f: (f, 0)),
            _resident((1, d)),
        ],
        out_specs=pl.BlockSpec((tm, d), lambda i, f: (i, 0)),
        out_shape=jax.ShapeDtypeStruct((n, d), F32),
        scratch_shapes=[pltpu.VMEM((tm, d), BF16)],
        compiler_params=_cparams("parallel", "arbitrary"),
        name="mlp",
    )(h, g, w1, w2, g_final)


def kernel(x, norm_mix_g, w_in, fox_b_f, pool_w, pool_scale, sgu_norm_g, sgu_w_s, sgu_b, ret_norm_g, w_out,
           norm_mlp_g, w_ff1, w_ff2, norm_final_g):
    batch, seq, d = x.shape
    depth = w_in.shape[0]
    n = batch * seq
    tm_in = min(512, n)
    tm_out = tm_mlp = min(1024, n)
    tf_mlp = 1024
    rows_per_step = seq
    fox_tiles = (min(FOX_TQ, seq), min(FOX_TK, seq), min(FOX_TD, seq), FOX_HEADS_PER_STEP)
    tables = _ret_tables(seq)

    w_in_b = w_in.astype(BF16)

    h = x.reshape(n, d)
    g_final = norm_final_g.reshape(1, d)
    for layer in range(depth):
        proj, flog_t = _inproj(h, norm_mix_g[layer].reshape(1, d), w_in_b, layer, tm_in)

        y_a = _pool(proj, pool_w[layer], pool_scale[layer].reshape(1, GROUP_WIDTH), batch, seq)
        sgu_bias = jnp.repeat(jnp.transpose(sgu_b[layer]), LANES, axis=1)
        y_b = _sgu(proj, sgu_norm_g[layer].reshape(1, GROUP_WIDTH), sgu_w_s[layer], sgu_bias, n, rows_per_step)
        bf_tab = jnp.zeros((8, LANES), F32).at[:N_HEADS].set(jnp.broadcast_to(fox_b_f[layer][:, None], (N_HEADS, LANES)))
        y_c, (w_out_b, w_ff1_b, w_ff2_b) = _fox(proj, flog_t, bf_tab, batch, seq, *fox_tiles,
                                                cast_weights=(w_out, w_ff1, w_ff2), layer=layer)
        y_d = _ret(proj, ret_norm_g[layer].reshape(1, GROUP_WIDTH), tables, batch, seq, rows_per_step)

        h = _outproj(h, (y_a, y_b, y_c, y_d), w_out_b, tm_out)
        h = _mlp(h, norm_mlp_g[layer].reshape(1, d), w_ff1_b, w_ff2_b, g_final, tm_mlp, tf_mlp,
                 final_norm=(layer == depth - 1))
    return h.reshape(batch, seq, d)
```

```python
import functools

import numpy as np
import jax
import jax.numpy as jnp
from jax import lax
from jax.experimental import pallas as pl
from jax.experimental.pallas import tpu as pltpu

F32 = jnp.float32
BF16 = jnp.bfloat16

EPS = 1e-6
NEG_BIG = -1e30
GROUP_WIDTH = 512
LANES = 128
N_HEADS = 4
POOL_WINDOWS = (2, 4, 8, 16)
CHUNK = 128
CHUNKS_PER_ITER = 4
RET_QK = 64
ROPE_BASE = 10000.0
N_MAIN = 9 * GROUP_WIDTH
F_COL0 = 6 * GROUP_WIDTH
W_PREP_COLS = 256
V7X_VMEM_LIMIT = 56 * 1024 * 1024
FOX_TQ, FOX_TK, FOX_TD, FOX_HEADS_PER_STEP = 512, 512, 512, 4


def _cparams(*sem):
    return pltpu.CompilerParams(dimension_semantics=sem, vmem_limit_bytes=V7X_VMEM_LIMIT)


def _resident(shape):
    nd = len(shape)
    return pl.BlockSpec(shape, lambda *_: (0,) * nd, pipeline_mode=pl.Buffered(1))


def _rms_scale(x):
    return lax.rsqrt(jnp.mean(x * x, axis=-1, keepdims=True) + EPS)


def _for_each_chunk(n_chunks, body):
    per_iter = min(CHUNKS_PER_ITER, n_chunks)
    assert n_chunks % per_iter == 0

    def group(gi, _):
        for ci in range(per_iter):
            body(pl.ds(pl.multiple_of((gi * per_iter + ci) * CHUNK, CHUNK), CHUNK))
        return 0

    lax.fori_loop(0, n_chunks // per_iter, group, 0)


def _wprep_kernel(w_ref, f_ref, w_out_ref, f_out_ref):
    t = pl.program_id(0)
    n_layers = w_ref.shape[1]
    half = RET_QK // 2

    def emit(row_groups):
        for layer in range(n_layers):
            for b, ranges in enumerate(row_groups):
                x = jnp.concatenate([w_ref[r0:r1, layer, :] for r0, r1 in ranges], axis=0)
                w_out_ref[layer, :, b * LANES:(b + 1) * LANES] = jnp.transpose(x).astype(BF16)

    n_blocks = W_PREP_COLS // LANES
    retention_qk = (t >= F_COL0 // W_PREP_COLS) & (t < (F_COL0 + 2 * N_HEADS * RET_QK) // W_PREP_COLS)

    @pl.when(jnp.logical_not(retention_qk))
    def _():
        emit([[(b * LANES, (b + 1) * LANES)] for b in range(n_blocks)])

    @pl.when(retention_qk)
    def _():
        emit([[(hd * RET_QK + hf * half, hd * RET_QK + (hf + 1) * half) for hd in range(N_HEADS)]
              for hf in range(n_blocks)])

    keep = lax.broadcasted_iota(jnp.int32, f_out_ref.shape[1:], 0) < N_HEADS
    for layer in range(n_layers):
        f_out_ref[layer] = jnp.where(keep, f_ref[:, layer, :], 0.0)


def _wprep(w_in):
    n_layers, d, _ = w_in.shape
    assert W_PREP_COLS == N_HEADS * RET_QK and F_COL0 % W_PREP_COLS == 0
    w_t = jnp.transpose(w_in, (2, 0, 1))

    def src_row(t):
        return t * W_PREP_COLS + jnp.where(t >= F_COL0 // W_PREP_COLS, N_HEADS, 0)

    return pl.pallas_call(
        _wprep_kernel,
        grid=(N_MAIN // W_PREP_COLS,),
        in_specs=[
            pl.BlockSpec((pl.Element(W_PREP_COLS), pl.Element(n_layers), pl.Element(d)), lambda t: (src_row(t), 0, 0)),
            pl.BlockSpec((pl.Element(8), pl.Element(n_layers), pl.Element(d)), lambda t: (F_COL0, 0, 0)),
        ],
        out_specs=[
            pl.BlockSpec((n_layers, d, W_PREP_COLS), lambda t: (0, 0, t)),
            pl.BlockSpec((n_layers, 8, d), lambda t: (0, 0, 0)),
        ],
        out_shape=[jax.ShapeDtypeStruct((n_layers, d, N_MAIN), BF16), jax.ShapeDtypeStruct((n_layers, 8, d), F32)],
        compiler_params=_cparams("arbitrary"),
        name="wprep",
    )(w_t, w_t)


def _inproj_kernel(h_ref, g_ref, w_ref, wf_ref, proj_ref, flog_ref):
    x = h_ref[...]
    xn = (x * _rms_scale(x) * g_ref[...]).astype(BF16)
    for c in range(N_MAIN // GROUP_WIDTH):
        cols = slice(c * GROUP_WIDTH, (c + 1) * GROUP_WIDTH)
        proj_ref[:, cols] = jnp.dot(xn, w_ref[:, cols], preferred_element_type=F32).astype(BF16)
    flog_ref[...] = lax.dot_general(wf_ref[...].astype(BF16), xn, (((1,), (1,)), ((), ())),
                                    preferred_element_type=F32)


def _inproj(h, g, w_main, wf, layer, tm):
    n, d = h.shape

    def layer_resident(shape):
        return pl.BlockSpec((None,) + shape, lambda i: (layer, 0, 0), pipeline_mode=pl.Buffered(1))

    return pl.pallas_call(
        _inproj_kernel,
        grid=(n // tm,),
        in_specs=[pl.BlockSpec((tm, d), lambda i: (i, 0)), _resident((1, d)),
                  layer_resident(w_main.shape[1:]), layer_resident(wf.shape[1:])],
        out_specs=[
            pl.BlockSpec((tm, N_MAIN), lambda i: (i, 0)),
            pl.BlockSpec((8, tm), lambda i: (0, i)),
        ],
        out_shape=[jax.ShapeDtypeStruct((n, N_MAIN), BF16), jax.ShapeDtypeStruct((8, n), F32)],
        compiler_params=_cparams("parallel"),
        name="inproj",
    )(h, g, w_main, wf)


def _pool_kernel(a_ref, pw_ref, ps_ref, o_ref):
    s_len = a_ref.shape[0]
    row = lax.broadcasted_iota(jnp.int32, (s_len, LANES), 0)
    for gi, w in enumerate(POOL_WINDOWS):
        cols = slice(gi * LANES, (gi + 1) * LANES)
        a = a_ref[:, cols].astype(F32)
        tot = a
        sh = 1
        while sh < w:
            tot = tot + jnp.where(row >= sh, pltpu.roll(tot, sh, axis=0), 0.0)
            sh *= 2
        cnt = jnp.minimum(row + 1, w).astype(F32)
        p = tot / cnt - a
        y = jnp.dot(p.astype(BF16), pw_ref[gi].astype(BF16), preferred_element_type=F32)
        o_ref[:, cols] = (y * ps_ref[:, cols]).astype(BF16)


def _pool(proj, pool_w, pool_scale, batch, seq):
    return pl.pallas_call(
        _pool_kernel,
        grid=(batch,),
        in_specs=[
            pl.BlockSpec((seq, GROUP_WIDTH), lambda b: (b, 0)),
            _resident(pool_w.shape),
            _resident((1, GROUP_WIDTH)),
        ],
        out_specs=pl.BlockSpec((seq, GROUP_WIDTH), lambda b: (b, 0)),
        out_shape=jax.ShapeDtypeStruct((batch * seq, GROUP_WIDTH), BF16),
        compiler_params=_cparams("parallel"),
        name="pool",
    )(proj, pool_w, pool_scale)


def _head_norm(x):
    mu = jnp.mean(x, axis=-1, keepdims=True)
    d = x - mu
    var = jnp.mean(d * d, axis=-1, keepdims=True)
    return d * lax.rsqrt(var + EPS)


def _sgu_kernel(u_ref, v_ref, ng_ref, w_ref, bias_ref, o_ref, *, n_chunks):
    r_i = lax.broadcasted_iota(jnp.int32, (CHUNK, CHUNK), 0)
    c_i = lax.broadcasted_iota(jnp.int32, (CHUNK, CHUNK), 1)
    w_causal = [jnp.where(c_i <= r_i, w_ref[h], 0.0).astype(BF16) for h in range(N_HEADS)]

    def chunk(rows):
        u = jax.nn.gelu(u_ref[rows, :].astype(F32))
        v = jax.nn.gelu(v_ref[rows, :].astype(F32))
        zs = []
        for h in range(N_HEADS):
            cols = slice(h * LANES, (h + 1) * LANES)
            vn = (_head_norm(v[:, cols]) * ng_ref[:, cols]).astype(BF16)
            zs.append(jnp.dot(w_causal[h], vn, preferred_element_type=F32))
        z = jnp.concatenate(zs, axis=1) + bias_ref[...]
        o_ref[rows, :] = (u * z).astype(BF16)

    _for_each_chunk(n_chunks, chunk)


def _sgu(proj, norm_g, w_s, bias_tab, n_rows, rows_per_step):
    n_chunks = rows_per_step // CHUNK
    return pl.pallas_call(
        functools.partial(_sgu_kernel, n_chunks=n_chunks),
        grid=(n_rows // rows_per_step,),
        in_specs=[
            pl.BlockSpec((rows_per_step, GROUP_WIDTH), lambda i: (i, 1)),
            pl.BlockSpec((rows_per_step, GROUP_WIDTH), lambda i: (i, 2)),
            _resident((1, GROUP_WIDTH)),
            _resident(w_s.shape),
            _resident((CHUNK, GROUP_WIDTH)),
        ],
        out_specs=pl.BlockSpec((rows_per_step, GROUP_WIDTH), lambda i: (i, 0)),
        out_shape=jax.ShapeDtypeStruct((n_rows, GROUP_WIDTH), BF16),
        compiler_params=_cparams("parallel"),
        name="sgu",
    )(proj, proj, norm_g, w_s, bias_tab)


def _fox_kernel(flog_ref, bf_ref, q_ref, k_ref, v_ref, *rest, tq, tk, td, hps, scale, n_cast):
    cast_src, o_ref, cast_dst, c_scr = rest[:n_cast], rest[n_cast], rest[n_cast + 1:2 * n_cast + 1], rest[-1]
    hg = pl.program_id(1)
    qi = pl.program_id(2)
    s_len = k_ref.shape[0]
    log2e = float(np.log2(np.e))

    for src, dst in zip(cast_src, cast_dst):
        dst[...] = src[...].astype(BF16)

    @pl.when((hg == 0) & (qi == 0))
    def _():
        x = flog_ref[...] + bf_ref[:, 0:1]
        c = jnp.minimum(x, 0.0) - jnp.log1p(jnp.exp(-jnp.abs(x)))
        lane = lax.broadcasted_iota(jnp.int32, c.shape, 1)
        sh = 1
        while sh < s_len:
            c = c + jnp.where(lane >= sh, pltpu.roll(c, sh, axis=1), 0.0)
            sh *= 2
        c = c * log2e
        for j in range(s_len // td):
            c_scr[j] = c[:, j * td:(j + 1) * td]

    heads = [slice(h * LANES, (h + 1) * LANES) for h in range(hps)]
    qs = [(q_ref[:, hs].astype(F32) * (scale * log2e)).astype(BF16) for hs in heads]
    row0 = qi * tq

    def update(state, q, kb, vb, ck, diagonal):
        m, l, acc = state
        s = lax.dot_general(q, kb, (((1,), (1,)), ((), ())), preferred_element_type=F32) - ck
        if diagonal:
            n = q.shape[0]
            own = s[:, s.shape[1] - n:]
            r_i = lax.broadcasted_iota(jnp.int32, own.shape, 0)
            c_i = lax.broadcasted_iota(jnp.int32, own.shape, 1)
            own = jnp.where(c_i <= r_i, own, NEG_BIG)
            s = own if n == s.shape[1] else jnp.concatenate([s[:, :s.shape[1] - n], own], axis=1)
        m_new = jnp.maximum(m, jnp.max(s, axis=-1, keepdims=True))
        alpha = jnp.exp2(m - m_new)
        p = jnp.exp2(s - m_new)
        l = alpha * l + jnp.sum(p, axis=-1, keepdims=True)
        acc = alpha * acc + jnp.dot(p.astype(BF16), vb, preferred_element_type=F32)
        return m_new, l, acc

    def bias_row(h, key_start, width):
        first = key_start // td
        parts = [c_scr[first + i, pl.ds(hg * hps + h, 1), :] for i in range(width // td)]
        return parts[0] if len(parts) == 1 else jnp.concatenate(parts, axis=1)

    def full_block(j, carry):
        start = pl.multiple_of(j * tk, tk)
        return tuple(
            update(carry[h], qs[h], k_ref[pl.ds(start, tk), hs], v_ref[pl.ds(start, tk), hs], bias_row(h, start, tk),
                   False)
            for h, hs in enumerate(heads))

    carry = tuple((jnp.full((tq, 1), NEG_BIG, F32), jnp.zeros((tq, 1), F32), jnp.zeros((tq, LANES), F32))
                  for _ in heads)
    carry = lax.fori_loop(0, row0 // tk, full_block, carry)

    start = pl.multiple_of(row0, tq)
    for h, hs in enumerate(heads):
        m, l, acc = carry[h]
        for r in range(tq // td):
            rows = slice(r * td, (r + 1) * td)
            n_keys = (r + 1) * td
            _, l_r, acc_r = update((m[rows], l[rows], acc[rows]), qs[h][rows], k_ref[pl.ds(start, n_keys), hs],
                                   v_ref[pl.ds(start, n_keys), hs], bias_row(h, start, n_keys), True)
            o_ref[rows, hs] = (acc_r / l_r).astype(BF16)


def _fox(proj, flog_t, bf_tab, batch, seq, tq, tk, td, hps, cast_weights, layer):
    assert tq % td == 0 and tk % td == 0 and tq % tk == 0
    nq = seq // tq
    width = hps * LANES
    per_piece = GROUP_WIDTH // width
    n_steps = batch * per_piece * nq

    def step(b, g, i):
        return (b * per_piece + g) * nq + i

    cast_in, cast_out, cast_shapes = [], [], []
    for w in cast_weights:
        _, rows, cols = w.shape
        slab = rows // n_steps
        assert slab * n_steps == rows and slab % 16 == 0, (w.shape, n_steps)
        cast_in.append(pl.BlockSpec((None, slab, cols), lambda b, g, i: (layer, step(b, g, i), 0)))
        cast_out.append(pl.BlockSpec((slab, cols), lambda b, g, i: (step(b, g, i), 0)))
        cast_shapes.append(jax.ShapeDtypeStruct((rows, cols), BF16))

    outs = pl.pallas_call(
        functools.partial(_fox_kernel, tq=tq, tk=tk, td=td, hps=hps, scale=LANES ** -0.5, n_cast=len(cast_weights)),
        grid=(batch, per_piece, nq),
        in_specs=[
            pl.BlockSpec((8, seq), lambda b, g, i: (0, b)),
            _resident((8, LANES)),
            pl.BlockSpec((tq, width), lambda b, g, i: (b * nq + i, 3 * per_piece + g)),
            pl.BlockSpec((seq, width), lambda b, g, i: (b, 4 * per_piece + g)),
            pl.BlockSpec((seq, width), lambda b, g, i: (b, 5 * per_piece + g)),
        ] + cast_in,
        out_specs=[pl.BlockSpec((tq, width), lambda b, g, i: (b * nq + i, g))] + cast_out,
        out_shape=[jax.ShapeDtypeStruct((batch * seq, GROUP_WIDTH), BF16)] + cast_shapes,
        scratch_shapes=[pltpu.VMEM((seq // td, 8, td), F32)],
        compiler_params=_cparams("parallel", "arbitrary", "arbitrary"),
        name="fox",
    )(flog_t, bf_tab, proj, proj, proj, *cast_weights)
    return outs[0], outs[1:]


def _ret_kernel(q_ref, k_ref, v_ref, g_ref, cos_ref, sin_ref, hm_ref, zeta_ref, decay_ref, xi_ref,
                cd_ref, bd_ref, ng_ref, o_ref, state_scr, *, n_chunks):
    @pl.when(pl.program_id(1) == 0)
    def _():
        state_scr[...] = jnp.zeros_like(state_scr)

    def rope(x, cos, sin):
        x1, x2 = x[:, :LANES], x[:, LANES:]
        return jnp.concatenate([x1 * cos - x2 * sin, x1 * sin + x2 * cos], axis=1)

    def chunk(rows):
        cos, sin = cos_ref[rows, :], sin_ref[rows, :]
        qr = rope(q_ref[rows, :].astype(F32), cos, sin)
        kr = rope(k_ref[rows, :].astype(F32), cos, sin) * (RET_QK ** -0.5)
        v = v_ref[rows, :]
        qb = qr.astype(BF16)
        k_heads = jnp.concatenate([(kr * hm_ref[h:h + 1, :]).astype(BF16) for h in range(N_HEADS)], axis=0)
        s = lax.dot_general(qb, k_heads, (((1,), (1,)), ((), ())), preferred_element_type=F32)
        sb = (s * decay_ref[...]).astype(BF16)
        y_intra = jnp.concatenate(
            [jnp.dot(sb[:, h * LANES:(h + 1) * LANES], v[:, h * LANES:(h + 1) * LANES], preferred_element_type=F32)
             for h in range(N_HEADS)], axis=1)
        state = state_scr[...]
        y_cross = jnp.dot(qb, state.astype(BF16), preferred_element_type=F32) * xi_ref[...]
        kz_t = jnp.transpose(kr * zeta_ref[...]).astype(BF16)
        upd = jnp.dot(kz_t, v, preferred_element_type=F32)
        state_scr[...] = state * cd_ref[...] + upd * bd_ref[...]
        y = y_intra + y_cross
        yn = jnp.concatenate([_head_norm(y[:, h * LANES:(h + 1) * LANES]) for h in range(N_HEADS)], axis=1)
        g = g_ref[rows, :].astype(F32)
        o_ref[rows, :] = (jax.nn.silu(g) * (yn * ng_ref[...])).astype(BF16)

    _for_each_chunk(n_chunks, chunk)


def _ret_tables(seq):
    half = RET_QK // 2
    inv = np.exp(-(np.arange(half, dtype=np.float32) / half) * np.float32(np.log(ROPE_BASE))).astype(np.float32)
    ang = np.arange(seq, dtype=np.float32)[:, None] * inv[None, :]
    cos = np.tile(np.cos(ang).astype(np.float32), (1, N_HEADS))
    sin = np.tile(np.sin(ang).astype(np.float32), (1, N_HEADS))
    gamma = (1.0 - np.exp((-5.0 - np.arange(N_HEADS, dtype=np.float32)) * np.float32(np.log(2.0)))).astype(np.float32)
    log_gamma = np.log(gamma).astype(np.float32)
    l = np.arange(CHUNK, dtype=np.float32)
    diff = l[:, None] - l[None, :]
    decay = np.where(diff[None] >= 0, np.exp(np.maximum(diff, 0.0)[None] * log_gamma[:, None, None]), 0.0)
    decay = np.transpose(decay, (1, 0, 2)).reshape(CHUNK, N_HEADS * CHUNK).astype(np.float32)
    xi = np.repeat(np.exp((l + 1.0)[:, None] * log_gamma[None, :]), LANES, axis=1).astype(np.float32)
    lane_head = np.tile(np.repeat(np.arange(N_HEADS), half), 2)
    zeta = np.exp((CHUNK - 1.0 - l)[:, None] * log_gamma[lane_head][None, :]).astype(np.float32)
    hm = np.zeros((8, 2 * LANES), np.float32)
    for h in range(N_HEADS):
        hm[h] = lane_head == h
    col_head = np.repeat(np.arange(N_HEADS), LANES)
    cd = np.exp(CHUNK * log_gamma)[col_head][None, :].astype(np.float32)
    bd = (lane_head[:, None] == col_head[None, :]).astype(np.float32)
    return tuple(jnp.asarray(t) for t in (cos, sin, hm, zeta, decay, xi, cd, bd))


def _ret(proj, norm_g, tables, batch, seq, rows_per_step):
    cos, sin, hm, zeta, decay, xi, cd, bd = tables
    n_chunks = rows_per_step // CHUNK
    steps = seq // rows_per_step
    half_blk = rows_per_step, 2 * LANES
    full_blk = rows_per_step, GROUP_WIDTH
    return pl.pallas_call(
        functools.partial(_ret_kernel, n_chunks=n_chunks),
        grid=(batch, steps),
        in_specs=[
            pl.BlockSpec(half_blk, lambda b, i: (b * steps + i, 12)),
            pl.BlockSpec(half_blk, lambda b, i: (b * steps + i, 13)),
            pl.BlockSpec(full_blk, lambda b, i: (b * steps + i, 7)),
            pl.BlockSpec(full_blk, lambda b, i: (b * steps + i, 8)),
            pl.BlockSpec((rows_per_step, LANES), lambda b, i: (i, 0)),
            pl.BlockSpec((rows_per_step, LANES), lambda b, i: (i, 0)),
            _resident(hm.shape), _resident(zeta.shape), _resident(decay.shape), _resident(xi.shape),
            _resident(cd.shape), _resident(bd.shape), _resident((1, GROUP_WIDTH)),
        ],
        out_specs=pl.BlockSpec(full_blk, lambda b, i: (b * steps + i, 0)),
        out_shape=jax.ShapeDtypeStruct((batch * seq, GROUP_WIDTH), BF16),
        scratch_shapes=[pltpu.VMEM((2 * LANES, GROUP_WIDTH), F32)],
        compiler_params=_cparams("parallel", "arbitrary"),
        name="retention",
    )(proj, proj, proj, proj, cos, sin, hm, zeta, decay, xi, cd, bd, norm_g)


def _outproj_kernel(h_ref, ya_ref, yb_ref, yc_ref, yd_ref, w_ref, o_ref, mix_scr):
    for gi, y_ref in enumerate((ya_ref, yb_ref, yc_ref, yd_ref)):
        mix_scr[:, gi * GROUP_WIDTH:(gi + 1) * GROUP_WIDTH] = y_ref[...]
    mix = mix_scr[...]
    for c in range(o_ref.shape[1] // GROUP_WIDTH):
        cols = slice(c * GROUP_WIDTH, (c + 1) * GROUP_WIDTH)
        o_ref[:, cols] = h_ref[:, cols] + jnp.dot(mix, w_ref[:, cols], preferred_element_type=F32)


def _outproj(h, ys, w_out, tm):
    n, d = h.shape
    y_spec = pl.BlockSpec((tm, GROUP_WIDTH), lambda i: (i, 0))
    w_spec = _resident(w_out.shape)
    return pl.pallas_call(
        _outproj_kernel,
        grid=(n // tm,),
        in_specs=[pl.BlockSpec((tm, d), lambda i: (i, 0)), y_spec, y_spec, y_spec, y_spec, w_spec],
        out_specs=pl.BlockSpec((tm, d), lambda i: (i, 0)),
        out_shape=jax.ShapeDtypeStruct((n, d), F32),
        scratch_shapes=[pltpu.VMEM((tm, N_HEADS * GROUP_WIDTH), BF16)],
        compiler_params=_cparams("parallel"),
        name="outproj",
    )(h, *ys, w_out)


def _mlp_kernel(h_ref, g_ref, w1_ref, w2_ref, gf_ref, o_ref, hn_scr, *, final_norm, n_split):
    f = pl.program_id(1)
    sub = h_ref.shape[0] // n_split
    row_groups = [slice(r * sub, (r + 1) * sub) for r in range(n_split)]

    @pl.when(f == 0)
    def _():
        for rows in row_groups:
            x = h_ref[rows, :]
            hn_scr[rows, :] = (x * _rms_scale(x) * g_ref[...]).astype(BF16)
            o_ref[rows, :] = x

    for rows in row_groups:
        a = jnp.dot(hn_scr[rows, :], w1_ref[...], preferred_element_type=F32)
        a = jnp.square(jnp.maximum(a, 0.0)).astype(BF16)
        o_ref[rows, :] += jnp.dot(a, w2_ref[...], preferred_element_type=F32)

    if final_norm:
        @pl.when(f == pl.num_programs(1) - 1)
        def _():
            for rows in row_groups:
                y = o_ref[rows, :]
                o_ref[rows, :] = y * _rms_scale(y) * gf_ref[...]


def _mlp(h, g, w1, w2, g_final, tm, tf, final_norm):
    n, d = h.shape
    d_ff = w1.shape[1]
    return pl.pallas_call(
        functools.partial(_mlp_kernel, final_norm=final_norm, n_split=max(1, tm // 512)),
        grid=(n // tm, d_ff // tf),
        in_specs=[
            pl.BlockSpec((tm, d), lambda i, f: (i, 0)),
            _resident((1, d)),
            pl.BlockSpec((d, tf), lambda i, f: (0, f)),
            pl.BlockSpec((tf, d), lambda i, f: (f, 0)),
            _resident((1, d)),
        ],
        out_specs=pl.BlockSpec((tm, d), lambda i, f: (i, 0)),
        out_shape=jax.ShapeDtypeStruct((n, d), F32),
        scratch_shapes=[pltpu.VMEM((tm, d), BF16)],
        compiler_params=_cparams("parallel", "arbitrary"),
        name="mlp",
    )(h, g, w1, w2, g_final)


def kernel(x, norm_mix_g, w_in, fox_b_f, pool_w, pool_scale, sgu_norm_g, sgu_w_s, sgu_b, ret_norm_g, w_out,
           norm_mlp_g, w_ff1, w_ff2, norm_final_g):
    batch, seq, d = x.shape
    depth = w_in.shape[0]
    n = batch * seq
    tm_in = min(512, n)
    tm_out = tm_mlp = min(1024, n)
    tf_mlp = 1024
    rows_per_step = seq
    fox_tiles = (min(FOX_TQ, seq), min(FOX_TK, seq), min(FOX_TD, seq), FOX_HEADS_PER_STEP)
    tables = _ret_tables(seq)

    w_main, w_forget = _wprep(w_in)

    h = x.reshape(n, d)
    g_final = norm_final_g.reshape(1, d)
    for layer in range(depth):
        proj, flog_t = _inproj(h, norm_mix_g[layer].reshape(1, d), w_main, w_forget, layer, tm_in)

        y_a = _pool(proj, pool_w[layer], pool_scale[layer].reshape(1, GROUP_WIDTH), batch, seq)
        sgu_bias = jnp.repeat(jnp.transpose(sgu_b[layer]), LANES, axis=1)
        y_b = _sgu(proj, sgu_norm_g[layer].reshape(1, GROUP_WIDTH), sgu_w_s[layer], sgu_bias, n, rows_per_step)
        bf_tab = jnp.zeros((8, LANES), F32).at[:N_HEADS].set(jnp.broadcast_to(fox_b_f[layer][:, None], (N_HEADS, LANES)))
        y_c, (w_out_b, w_ff1_b, w_ff2_b) = _fox(proj, flog_t, bf_tab, batch, seq, *fox_tiles,
                                                cast_weights=(w_out, w_ff1, w_ff2), layer=layer)
        y_d = _ret(proj, ret_norm_g[layer].reshape(1, GROUP_WIDTH), tables, batch, seq, rows_per_step)

        h = _outproj(h, (y_a, y_b, y_c, y_d), w_out_b, tm_out)
        h = _mlp(h, norm_mlp_g[layer].reshape(1, d), w_ff1_b, w_ff2_b, g_final, tm_mlp, tf_mlp,
                 final_norm=(layer == depth - 1))
    return h.reshape(batch, seq, d)
```

```python
import functools

import numpy as np
import jax
import jax.numpy as jnp
from jax import lax
from jax.experimental import pallas as pl
from jax.experimental.pallas import tpu as pltpu

F32 = jnp.float32
BF16 = jnp.bfloat16

EPS = 1e-6
NEG_BIG = -1e30
GROUP_WIDTH = 512
LANES = 128
N_HEADS = 4
POOL_WINDOWS = (2, 4, 8, 16)
CHUNK = 128
POOL_HALO = 16
RET_QK = 64
ROPE_BASE = 10000.0
N_MAIN = 9 * GROUP_WIDTH
F_COL0 = 6 * GROUP_WIDTH
W_PREP_COLS = 256
V7X_VMEM_LIMIT = 56 * 1024 * 1024
FOX_TQ, FOX_TK, FOX_TD, FOX_HEADS_PER_STEP = 512, 512, 512, 4


def _cparams(*sem):
    return pltpu.CompilerParams(dimension_semantics=sem, vmem_limit_bytes=V7X_VMEM_LIMIT)


def _resident(shape):
    nd = len(shape)
    return pl.BlockSpec(shape, lambda *_: (0,) * nd, pipeline_mode=pl.Buffered(1))


def _rms_scale(x):
    return lax.rsqrt(jnp.mean(x * x, axis=-1, keepdims=True) + EPS)


def _wprep_kernel(w_ref, f_ref, w_out_ref, f_out_ref):
    t = pl.program_id(0)
    n_layers = w_ref.shape[1]
    half = RET_QK // 2

    def emit(row_groups):
        for layer in range(n_layers):
            for b, ranges in enumerate(row_groups):
                x = jnp.concatenate([w_ref[r0:r1, layer, :] for r0, r1 in ranges], axis=0)
                w_out_ref[layer, :, b * LANES:(b + 1) * LANES] = jnp.transpose(x).astype(BF16)

    n_blocks = W_PREP_COLS // LANES
    retention_qk = (t >= F_COL0 // W_PREP_COLS) & (t < (F_COL0 + 2 * N_HEADS * RET_QK) // W_PREP_COLS)

    @pl.when(jnp.logical_not(retention_qk))
    def _():
        emit([[(b * LANES, (b + 1) * LANES)] for b in range(n_blocks)])

    @pl.when(retention_qk)
    def _():
        emit([[(hd * RET_QK + hf * half, hd * RET_QK + (hf + 1) * half) for hd in range(N_HEADS)]
              for hf in range(n_blocks)])

    keep = lax.broadcasted_iota(jnp.int32, f_out_ref.shape[1:], 0) < N_HEADS
    for layer in range(n_layers):
        f_out_ref[layer] = jnp.where(keep, f_ref[:, layer, :], 0.0)


def _wprep(w_in):
    n_layers, d, _ = w_in.shape
    assert W_PREP_COLS == N_HEADS * RET_QK and F_COL0 % W_PREP_COLS == 0
    w_t = jnp.transpose(w_in, (2, 0, 1))

    def src_row(t):
        return t * W_PREP_COLS + jnp.where(t >= F_COL0 // W_PREP_COLS, N_HEADS, 0)

    return pl.pallas_call(
        _wprep_kernel,
        grid=(N_MAIN // W_PREP_COLS,),
        in_specs=[
            pl.BlockSpec((pl.Element(W_PREP_COLS), pl.Element(n_layers), pl.Element(d)), lambda t: (src_row(t), 0, 0)),
            pl.BlockSpec((pl.Element(8), pl.Element(n_layers), pl.Element(d)), lambda t: (F_COL0, 0, 0)),
        ],
        out_specs=[
            pl.BlockSpec((n_layers, d, W_PREP_COLS), lambda t: (0, 0, t)),
            pl.BlockSpec((n_layers, 8, d), lambda t: (0, 0, 0)),
        ],
        out_shape=[jax.ShapeDtypeStruct((n_layers, d, N_MAIN), BF16), jax.ShapeDtypeStruct((n_layers, 8, d), F32)],
        compiler_params=_cparams("arbitrary"),
        name="wprep",
    )(w_t, w_t)


def _head_norm(x):
    mu = jnp.mean(x, axis=-1, keepdims=True)
    d = x - mu
    var = jnp.mean(d * d, axis=-1, keepdims=True)
    return d * lax.rsqrt(var + EPS)


def _pool_tile(a, halo, pos0, pw_ref, ps_ref):
    tm = a.shape[0]
    ext = jnp.concatenate([halo, a], axis=0)
    pos = lax.broadcasted_iota(jnp.int32, (tm, LANES), 0) + pos0
    outs = []
    for gi, w in enumerate(POOL_WINDOWS):
        cols = slice(gi * LANES, (gi + 1) * LANES)
        tot = ext[:, cols]
        sh = 1
        while sh < w:
            tot = tot + pltpu.roll(tot, sh, axis=0)
            sh *= 2
        cnt = jnp.minimum(pos + 1, w).astype(F32)
        p = tot[POOL_HALO:] / cnt - a[:, cols]
        y = jnp.dot(p.astype(BF16), pw_ref[gi].astype(BF16), preferred_element_type=F32)
        outs.append(y * ps_ref[:, cols])
    return jnp.concatenate(outs, axis=1)


def _sgu_chunk(u, v, w_causal, ng_ref, bias_ref):
    u = jax.nn.gelu(u)
    v = jax.nn.gelu(v)
    zs = []
    for h in range(N_HEADS):
        cols = slice(h * LANES, (h + 1) * LANES)
        vn = (_head_norm(v[:, cols]) * ng_ref[:, cols]).astype(BF16)
        zs.append(jnp.dot(w_causal[h], vn, preferred_element_type=F32))
    return u * (jnp.concatenate(zs, axis=1) + bias_ref[...])


def _rope(x, cos, sin):
    x1, x2 = x[:, :LANES], x[:, LANES:]
    return jnp.concatenate([x1 * cos - x2 * sin, x1 * sin + x2 * cos], axis=1)


def _retention_chunk(q, k, v, g, cos, sin, state_scr, hm_ref, zeta_ref, decay_ref, xi_ref, cd_ref, bd_ref, ng_ref):
    qb = _rope(q, cos, sin).astype(BF16)
    kr = _rope(k, cos, sin) * (RET_QK ** -0.5)
    k_heads = jnp.concatenate([(kr * hm_ref[h:h + 1, :]).astype(BF16) for h in range(N_HEADS)], axis=0)
    s = lax.dot_general(qb, k_heads, (((1,), (1,)), ((), ())), preferred_element_type=F32)
    sb = (s * decay_ref[...]).astype(BF16)
    y_intra = jnp.concatenate(
        [jnp.dot(sb[:, h * LANES:(h + 1) * LANES], v[:, h * LANES:(h + 1) * LANES], preferred_element_type=F32)
         for h in range(N_HEADS)], axis=1)
    state = state_scr[...]
    y_cross = jnp.dot(qb, state.astype(BF16), preferred_element_type=F32) * xi_ref[...]
    kz_t = jnp.transpose(kr * zeta_ref[...]).astype(BF16)
    upd = jnp.dot(kz_t, v, preferred_element_type=F32)
    state_scr[...] = state * cd_ref[...] + upd * bd_ref[...]
    y = y_intra + y_cross
    yn = jnp.concatenate([_head_norm(y[:, h * LANES:(h + 1) * LANES]) for h in range(N_HEADS)], axis=1)
    return jax.nn.silu(g) * (yn * ng_ref[...])


def _inmix_kernel(h_ref, g_ref, w_ref, wf_ref, pool_w_ref, pool_s_ref, sgu_g_ref, sgu_w_ref, sgu_b_ref,
                  cos_ref, sin_ref, hm_ref, zeta_ref, decay_ref, xi_ref, cd_ref, bd_ref, ret_g_ref,
                  qkv_ref, flog_ref, ya_ref, yb_ref, yd_ref, state_scr, halo_scr, *, tiles_per_seq):
    tm = h_ref.shape[0]
    tile_in_seq = pl.program_id(0) % tiles_per_seq

    @pl.when(tile_in_seq == 0)
    def _():
        state_scr[...] = jnp.zeros_like(state_scr)
        halo_scr[...] = jnp.zeros_like(halo_scr)

    x = h_ref[...]
    xn = (x * _rms_scale(x) * g_ref[...]).astype(BF16)

    def project(piece):
        cols = slice(piece * GROUP_WIDTH, (piece + 1) * GROUP_WIDTH)
        return jnp.dot(xn, w_ref[:, cols], preferred_element_type=F32)

    chunks = [slice(ci * CHUNK, (ci + 1) * CHUNK) for ci in range(tm // CHUNK)]

    qk, rv, rg = project(6), project(7).astype(BF16), project(8)
    for rows in chunks:
        yd_ref[rows, :] = _retention_chunk(
            qk[rows, :2 * LANES], qk[rows, 2 * LANES:], rv[rows], rg[rows], cos_ref[rows, :], sin_ref[rows, :],
            state_scr, hm_ref, zeta_ref, decay_ref, xi_ref, cd_ref, bd_ref, ret_g_ref).astype(BF16)

    u, v = project(1), project(2)
    r_i = lax.broadcasted_iota(jnp.int32, (CHUNK, CHUNK), 0)
    c_i = lax.broadcasted_iota(jnp.int32, (CHUNK, CHUNK), 1)
    w_causal = [jnp.where(c_i <= r_i, sgu_w_ref[h], 0.0).astype(BF16) for h in range(N_HEADS)]
    for rows in chunks:
        yb_ref[rows, :] = _sgu_chunk(u[rows], v[rows], w_causal, sgu_g_ref, sgu_b_ref).astype(BF16)

    a = project(0)
    ya_ref[...] = _pool_tile(a, halo_scr[...], tile_in_seq * tm, pool_w_ref, pool_s_ref).astype(BF16)
    halo_scr[...] = a[tm - POOL_HALO:, :]

    for piece in range(3):
        qkv_ref[:, piece * GROUP_WIDTH:(piece + 1) * GROUP_WIDTH] = project(3 + piece).astype(BF16)

    flog_ref[...] = lax.dot_general(wf_ref[...].astype(BF16), xn, (((1,), (1,)), ((), ())),
                                    preferred_element_type=F32)


def _inmix(h, g, w_main, wf, layer, pool_w, pool_scale, sgu_g, sgu_w, sgu_bias, ret_tables, ret_g, seq, tm):
    n, d = h.shape
    assert seq % tm == 0 and tm % CHUNK == 0
    tiles_per_seq = seq // tm
    cos, sin, hm, zeta, decay, xi, cd, bd = ret_tables

    def layer_resident(shape):
        return pl.BlockSpec((None,) + shape, lambda i: (layer, 0, 0), pipeline_mode=pl.Buffered(1))

    def rows(width):
        return pl.BlockSpec((tm, width), lambda i: (i, 0))

    pos_rows = pl.BlockSpec((tm, LANES), lambda i: (i % tiles_per_seq, 0))
    mix_out = jax.ShapeDtypeStruct((n, GROUP_WIDTH), BF16)
    return pl.pallas_call(
        functools.partial(_inmix_kernel, tiles_per_seq=tiles_per_seq),
        grid=(n // tm,),
        in_specs=[rows(d), _resident((1, d)), layer_resident(w_main.shape[1:]), layer_resident(wf.shape[1:]),
                  _resident(pool_w.shape), _resident((1, GROUP_WIDTH)),
                  _resident((1, GROUP_WIDTH)), _resident(sgu_w.shape), _resident((CHUNK, GROUP_WIDTH)),
                  pos_rows, pos_rows, _resident(hm.shape), _resident(zeta.shape), _resident(decay.shape),
                  _resident(xi.shape), _resident(cd.shape), _resident(bd.shape), _resident((1, GROUP_WIDTH))],
        out_specs=[rows(3 * GROUP_WIDTH), pl.BlockSpec((8, tm), lambda i: (0, i)),
                   rows(GROUP_WIDTH), rows(GROUP_WIDTH), rows(GROUP_WIDTH)],
        out_shape=[jax.ShapeDtypeStruct((n, 3 * GROUP_WIDTH), BF16), jax.ShapeDtypeStruct((8, n), F32),
                   mix_out, mix_out, mix_out],
        scratch_shapes=[pltpu.VMEM((2 * LANES, GROUP_WIDTH), F32), pltpu.VMEM((POOL_HALO, GROUP_WIDTH), F32)],
        compiler_params=_cparams("arbitrary"),
        name="inmix",
    )(h, g, w_main, wf, pool_w, pool_scale, sgu_g, sgu_w, sgu_bias, cos, sin, hm, zeta, decay, xi, cd, bd, ret_g)


def _fox_kernel(flog_ref, bf_ref, q_ref, k_ref, v_ref, *rest, tq, tk, td, hps, scale, n_cast):
    cast_src, o_ref, cast_dst, c_scr = rest[:n_cast], rest[n_cast], rest[n_cast + 1:2 * n_cast + 1], rest[-1]
    hg = pl.program_id(1)
    qi = pl.program_id(2)
    s_len = k_ref.shape[0]
    log2e = float(np.log2(np.e))

    for src, dst in zip(cast_src, cast_dst):
        dst[...] = src[...].astype(BF16)

    @pl.when((hg == 0) & (qi == 0))
    def _():
        x = flog_ref[...] + bf_ref[:, 0:1]
        c = jnp.minimum(x, 0.0) - jnp.log1p(jnp.exp(-jnp.abs(x)))
        lane = lax.broadcasted_iota(jnp.int32, c.shape, 1)
        sh = 1
        while sh < s_len:
            c = c + jnp.where(lane >= sh, pltpu.roll(c, sh, axis=1), 0.0)
            sh *= 2
        c = c * log2e
        for j in range(s_len // td):
            c_scr[j] = c[:, j * td:(j + 1) * td]

    heads = [slice(h * LANES, (h + 1) * LANES) for h in range(hps)]
    qs = [(q_ref[:, hs].astype(F32) * (scale * log2e)).astype(BF16) for hs in heads]
    row0 = qi * tq

    def update(state, q, kb, vb, ck, diagonal):
        m, l, acc = state
        s = lax.dot_general(q, kb, (((1,), (1,)), ((), ())), preferred_element_type=F32) - ck
        if diagonal:
            n = q.shape[0]
            own = s[:, s.shape[1] - n:]
            r_i = lax.broadcasted_iota(jnp.int32, own.shape, 0)
            c_i = lax.broadcasted_iota(jnp.int32, own.shape, 1)
            own = jnp.where(c_i <= r_i, own, NEG_BIG)
            s = own if n == s.shape[1] else jnp.concatenate([s[:, :s.shape[1] - n], own], axis=1)
        m_new = jnp.maximum(m, jnp.max(s, axis=-1, keepdims=True))
        alpha = jnp.exp2(m - m_new)
        p = jnp.exp2(s - m_new)
        l = alpha * l + jnp.sum(p, axis=-1, keepdims=True)
        acc = alpha * acc + jnp.dot(p.astype(BF16), vb, preferred_element_type=F32)
        return m_new, l, acc

    def bias_row(h, key_start, width):
        first = key_start // td
        parts = [c_scr[first + i, pl.ds(hg * hps + h, 1), :] for i in range(width // td)]
        return parts[0] if len(parts) == 1 else jnp.concatenate(parts, axis=1)

    def full_block(j, carry):
        start = pl.multiple_of(j * tk, tk)
        return tuple(
            update(carry[h], qs[h], k_ref[pl.ds(start, tk), hs], v_ref[pl.ds(start, tk), hs], bias_row(h, start, tk),
                   False)
            for h, hs in enumerate(heads))

    carry = tuple((jnp.full((tq, 1), NEG_BIG, F32), jnp.zeros((tq, 1), F32), jnp.zeros((tq, LANES), F32))
                  for _ in heads)
    carry = lax.fori_loop(0, row0 // tk, full_block, carry)

    start = pl.multiple_of(row0, tq)
    for h, hs in enumerate(heads):
        m, l, acc = carry[h]
        for r in range(tq // td):
            rows = slice(r * td, (r + 1) * td)
            n_keys = (r + 1) * td
            _, l_r, acc_r = update((m[rows], l[rows], acc[rows]), qs[h][rows], k_ref[pl.ds(start, n_keys), hs],
                                   v_ref[pl.ds(start, n_keys), hs], bias_row(h, start, n_keys), True)
            o_ref[rows, hs] = (acc_r / l_r).astype(BF16)


def _fox(qkv, flog_t, bf_tab, batch, seq, tq, tk, td, hps, cast_weights, layer):
    assert tq % td == 0 and tk % td == 0 and tq % tk == 0
    nq = seq // tq
    width = hps * LANES
    per_piece = GROUP_WIDTH // width
    n_steps = batch * per_piece * nq

    def step(b, g, i):
        return (b * per_piece + g) * nq + i

    cast_in, cast_out, cast_shapes = [], [], []
    for w in cast_weights:
        _, rows, cols = w.shape
        slab = rows // n_steps
        assert slab * n_steps == rows and slab % 16 == 0, (w.shape, n_steps)
        cast_in.append(pl.BlockSpec((None, slab, cols), lambda b, g, i: (layer, step(b, g, i), 0)))
        cast_out.append(pl.BlockSpec((slab, cols), lambda b, g, i: (step(b, g, i), 0)))
        cast_shapes.append(jax.ShapeDtypeStruct((rows, cols), BF16))

    outs = pl.pallas_call(
        functools.partial(_fox_kernel, tq=tq, tk=tk, td=td, hps=hps, scale=LANES ** -0.5, n_cast=len(cast_weights)),
        grid=(batch, per_piece, nq),
        in_specs=[
            pl.BlockSpec((8, seq), lambda b, g, i: (0, b)),
            _resident((8, LANES)),
            pl.BlockSpec((tq, width), lambda b, g, i: (b * nq + i, g)),
            pl.BlockSpec((seq, width), lambda b, g, i: (b, per_piece + g)),
            pl.BlockSpec((seq, width), lambda b, g, i: (b, 2 * per_piece + g)),
        ] + cast_in,
        out_specs=[pl.BlockSpec((tq, width), lambda b, g, i: (b * nq + i, g))] + cast_out,
        out_shape=[jax.ShapeDtypeStruct((batch * seq, GROUP_WIDTH), BF16)] + cast_shapes,
        scratch_shapes=[pltpu.VMEM((seq // td, 8, td), F32)],
        compiler_params=_cparams("parallel", "arbitrary", "arbitrary"),
        name="fox",
    )(flog_t, bf_tab, qkv, qkv, qkv, *cast_weights)
    return outs[0], outs[1:]


def _ret_tables(seq):
    half = RET_QK // 2
    inv = np.exp(-(np.arange(half, dtype=np.float32) / half) * np.float32(np.log(ROPE_BASE))).astype(np.float32)
    ang = np.arange(seq, dtype=np.float32)[:, None] * inv[None, :]
    cos = np.tile(np.cos(ang).astype(np.float32), (1, N_HEADS))
    sin = np.tile(np.sin(ang).astype(np.float32), (1, N_HEADS))
    gamma = (1.0 - np.exp((-5.0 - np.arange(N_HEADS, dtype=np.float32)) * np.float32(np.log(2.0)))).astype(np.float32)
    log_gamma = np.log(gamma).astype(np.float32)
    l = np.arange(CHUNK, dtype=np.float32)
    diff = l[:, None] - l[None, :]
    decay = np.where(diff[None] >= 0, np.exp(np.maximum(diff, 0.0)[None] * log_gamma[:, None, None]), 0.0)
    decay = np.transpose(decay, (1, 0, 2)).reshape(CHUNK, N_HEADS * CHUNK).astype(np.float32)
    xi = np.repeat(np.exp((l + 1.0)[:, None] * log_gamma[None, :]), LANES, axis=1).astype(np.float32)
    lane_head = np.tile(np.repeat(np.arange(N_HEADS), half), 2)
    zeta = np.exp((CHUNK - 1.0 - l)[:, None] * log_gamma[lane_head][None, :]).astype(np.float32)
    hm = np.zeros((8, 2 * LANES), np.float32)
    for h in range(N_HEADS):
        hm[h] = lane_head == h
    col_head = np.repeat(np.arange(N_HEADS), LANES)
    cd = np.exp(CHUNK * log_gamma)[col_head][None, :].astype(np.float32)
    bd = (lane_head[:, None] == col_head[None, :]).astype(np.float32)
    return tuple(jnp.asarray(t) for t in (cos, sin, hm, zeta, decay, xi, cd, bd))


def _outproj_kernel(h_ref, ya_ref, yb_ref, yc_ref, yd_ref, w_ref, o_ref, mix_scr):
    for gi, y_ref in enumerate((ya_ref, yb_ref, yc_ref, yd_ref)):
        mix_scr[:, gi * GROUP_WIDTH:(gi + 1) * GROUP_WIDTH] = y_ref[...]
    mix = mix_scr[...]
    for c in range(o_ref.shape[1] // GROUP_WIDTH):
        cols = slice(c * GROUP_WIDTH, (c + 1) * GROUP_WIDTH)
        o_ref[:, cols] = h_ref[:, cols] + jnp.dot(mix, w_ref[:, cols], preferred_element_type=F32)


def _outproj(h, ys, w_out, tm):
    n, d = h.shape
    y_spec = pl.BlockSpec((tm, GROUP_WIDTH), lambda i: (i, 0))
    w_spec = _resident(w_out.shape)
    return pl.pallas_call(
        _outproj_kernel,
        grid=(n // tm,),
        in_specs=[pl.BlockSpec((tm, d), lambda i: (i, 0)), y_spec, y_spec, y_spec, y_spec, w_spec],
        out_specs=pl.BlockSpec((tm, d), lambda i: (i, 0)),
        out_shape=jax.ShapeDtypeStruct((n, d), F32),
        scratch_shapes=[pltpu.VMEM((tm, N_HEADS * GROUP_WIDTH), BF16)],
        compiler_params=_cparams("parallel"),
        name="outproj",
    )(h, *ys, w_out)


def _mlp_kernel(h_ref, g_ref, w1_ref, w2_ref, gf_ref, o_ref, hn_scr, *, final_norm, n_split):
    f = pl.program_id(1)
    sub = h_ref.shape[0] // n_split
    row_groups = [slice(r * sub, (r + 1) * sub) for r in range(n_split)]

    @pl.when(f == 0)
    def _():
        for rows in row_groups:
            x = h_ref[rows, :]
            hn_scr[rows, :] = (x * _rms_scale(x) * g_ref[...]).astype(BF16)
            o_ref[rows, :] = x

    for rows in row_groups:
        a = jnp.dot(hn_scr[rows, :], w1_ref[...], preferred_element_type=F32)
        a = jnp.square(jnp.maximum(a, 0.0)).astype(BF16)
        o_ref[rows, :] += jnp.dot(a, w2_ref[...], preferred_element_type=F32)

    if final_norm:
        @pl.when(f == pl.num_programs(1) - 1)
        def _():
            for rows in row_groups:
                y = o_ref[rows, :]
                o_ref[rows, :] = y * _rms_scale(y) * gf_ref[...]


def _mlp(h, g, w1, w2, g_final, tm, tf, final_norm):
    n, d = h.shape
    d_ff = w1.shape[1]
    return pl.pallas_call(
        functools.partial(_mlp_kernel, final_norm=final_norm, n_split=max(1, tm // 512)),
        grid=(n // tm, d_ff // tf),
        in_specs=[
            pl.BlockSpec((tm, d), lambda i, f: (i, 0)),
            _resident((1, d)),
            pl.BlockSpec((d, tf), lambda i, f: (0, f)),
            pl.BlockSpec((tf, d), lambda i, f: (f, 0)),
            _resident((1, d)),
        ],
        out_specs=pl.BlockSpec((tm, d), lambda i, f: (i, 0)),
        out_shape=jax.ShapeDtypeStruct((n, d), F32),
        scratch_shapes=[pltpu.VMEM((tm, d), BF16)],
        compiler_params=_cparams("parallel", "arbitrary"),
        name="mlp",
    )(h, g, w1, w2, g_final)


def kernel(x, norm_mix_g, w_in, fox_b_f, pool_w, pool_scale, sgu_norm_g, sgu_w_s, sgu_b, ret_norm_g, w_out,
           norm_mlp_g, w_ff1, w_ff2, norm_final_g):
    batch, seq, d = x.shape
    depth = w_in.shape[0]
    n = batch * seq
    tm_in = min(512, seq)
    tm_out = tm_mlp = min(1024, n)
    tf_mlp = 1024
    fox_tiles = (min(FOX_TQ, seq), min(FOX_TK, seq), min(FOX_TD, seq), FOX_HEADS_PER_STEP)
    tables = _ret_tables(seq)

    w_main, w_forget = _wprep(w_in)

    h = x.reshape(n, d)
    g_final = norm_final_g.reshape(1, d)
    for layer in range(depth):
        sgu_bias = jnp.repeat(jnp.transpose(sgu_b[layer]), LANES, axis=1)
        qkv, flog_t, y_a, y_b, y_d = _inmix(
            h, norm_mix_g[layer].reshape(1, d), w_main, w_forget, layer,
            pool_w[layer], pool_scale[layer].reshape(1, GROUP_WIDTH),
            sgu_norm_g[layer].reshape(1, GROUP_WIDTH), sgu_w_s[layer], sgu_bias,
            tables, ret_norm_g[layer].reshape(1, GROUP_WIDTH), seq, tm_in)
        bf_tab = jnp.zeros((8, LANES), F32).at[:N_HEADS].set(jnp.broadcast_to(fox_b_f[layer][:, None], (N_HEADS, LANES)))
        y_c, (w_out_b, w_ff1_b, w_ff2_b) = _fox(qkv, flog_t, bf_tab, batch, seq, *fox_tiles,
                                                cast_weights=(w_out, w_ff1, w_ff2), layer=layer)

        h = _outproj(h, (y_a, y_b, y_c, y_d), w_out_b, tm_out)
        h = _mlp(h, norm_mlp_g[layer].reshape(1, d), w_ff1_b, w_ff2_b, g_final, tm_mlp, tf_mlp,
                 final_norm=(layer == depth - 1))
    return h.reshape(batch, seq, d)
```

```python
import functools

import numpy as np
import jax
import jax.numpy as jnp
from jax import lax
from jax.experimental import pallas as pl
from jax.experimental.pallas import tpu as pltpu

F32 = jnp.float32
BF16 = jnp.bfloat16

EPS = 1e-6
NEG_BIG = -1e30
GROUP_WIDTH = 512
LANES = 128
N_HEADS = 4
POOL_WINDOWS = (2, 4, 8, 16)
CHUNK = 128
POOL_HALO = 16
RET_QK = 64
ROPE_BASE = 10000.0
N_MAIN = 9 * GROUP_WIDTH
F_COL0 = 6 * GROUP_WIDTH
W_PREP_COLS = 256
V7X_VMEM_LIMIT = 56 * 1024 * 1024
FOX_TQ, FOX_TK, FOX_TD, FOX_HEADS_PER_STEP = 512, 512, 512, 4


def _cparams(*sem):
    return pltpu.CompilerParams(dimension_semantics=sem, vmem_limit_bytes=V7X_VMEM_LIMIT)


def _resident(shape):
    nd = len(shape)
    return pl.BlockSpec(shape, lambda *_: (0,) * nd, pipeline_mode=pl.Buffered(1))


def _rms_scale(x):
    return lax.rsqrt(jnp.mean(x * x, axis=-1, keepdims=True) + EPS)


def _wprep_kernel(w_ref, f_ref, w_out_ref, f_out_ref):
    t = pl.program_id(0)
    n_layers = w_ref.shape[1]
    half = RET_QK // 2

    def emit(row_groups):
        for layer in range(n_layers):
            for b, ranges in enumerate(row_groups):
                x = jnp.concatenate([w_ref[r0:r1, layer, :] for r0, r1 in ranges], axis=0)
                w_out_ref[layer, :, b * LANES:(b + 1) * LANES] = jnp.transpose(x).astype(BF16)

    n_blocks = W_PREP_COLS // LANES
    retention_qk = (t >= F_COL0 // W_PREP_COLS) & (t < (F_COL0 + 2 * N_HEADS * RET_QK) // W_PREP_COLS)

    @pl.when(jnp.logical_not(retention_qk))
    def _():
        emit([[(b * LANES, (b + 1) * LANES)] for b in range(n_blocks)])

    @pl.when(retention_qk)
    def _():
        emit([[(hd * RET_QK + hf * half, hd * RET_QK + (hf + 1) * half) for hd in range(N_HEADS)]
              for hf in range(n_blocks)])

    keep = lax.broadcasted_iota(jnp.int32, f_out_ref.shape[1:], 0) < N_HEADS
    for layer in range(n_layers):
        f_out_ref[layer] = jnp.where(keep, f_ref[:, layer, :], 0.0)


def _wprep(w_in):
    n_layers, d, _ = w_in.shape
    assert W_PREP_COLS == N_HEADS * RET_QK and F_COL0 % W_PREP_COLS == 0
    w_t = jnp.transpose(w_in, (2, 0, 1))

    def src_row(t):
        return t * W_PREP_COLS + jnp.where(t >= F_COL0 // W_PREP_COLS, N_HEADS, 0)

    return pl.pallas_call(
        _wprep_kernel,
        grid=(N_MAIN // W_PREP_COLS,),
        in_specs=[
            pl.BlockSpec((pl.Element(W_PREP_COLS), pl.Element(n_layers), pl.Element(d)), lambda t: (src_row(t), 0, 0)),
            pl.BlockSpec((pl.Element(8), pl.Element(n_layers), pl.Element(d)), lambda t: (F_COL0, 0, 0)),
        ],
        out_specs=[
            pl.BlockSpec((n_layers, d, W_PREP_COLS), lambda t: (0, 0, t)),
            pl.BlockSpec((n_layers, 8, d), lambda t: (0, 0, 0)),
        ],
        out_shape=[jax.ShapeDtypeStruct((n_layers, d, N_MAIN), BF16), jax.ShapeDtypeStruct((n_layers, 8, d), F32)],
        compiler_params=_cparams("arbitrary"),
        name="wprep",
    )(w_t, w_t)


def _head_norm(x):
    mu = jnp.mean(x, axis=-1, keepdims=True)
    d = x - mu
    var = jnp.mean(d * d, axis=-1, keepdims=True)
    return d * lax.rsqrt(var + EPS)


def _pool_windows(a, halo, pos0):
    tm = a.shape[0]
    ext = jnp.concatenate([halo, a], axis=0)
    pos = lax.broadcasted_iota(jnp.int32, (tm, LANES), 0) + pos0
    ps = []
    for gi, w in enumerate(POOL_WINDOWS):
        cols = slice(gi * LANES, (gi + 1) * LANES)
        tot = ext[:, cols]
        sh = 1
        while sh < w:
            tot = tot + pltpu.roll(tot, sh, axis=0)
            sh *= 2
        cnt = jnp.minimum(pos + 1, w).astype(F32)
        ps.append((tot[POOL_HALO:] / cnt - a[:, cols]).astype(BF16))
    return ps


def _pool_mix(ps, pw_ref, ps_ref):
    ys = [jnp.dot(p, pw_ref[gi].astype(BF16), preferred_element_type=F32) for gi, p in enumerate(ps)]
    return jnp.concatenate(ys, axis=1) * ps_ref[...]


def _sgu_gates(u, v, ng_ref):
    v = jax.nn.gelu(v)
    vns = [(_head_norm(v[:, h * LANES:(h + 1) * LANES]) * ng_ref[:, h * LANES:(h + 1) * LANES]).astype(BF16)
           for h in range(N_HEADS)]
    return jax.nn.gelu(u), vns


def _sgu_mix(gates, w_causal, bias_ref):
    zs = [jnp.dot(w_causal[h], jnp.concatenate([vns[h] for _, vns in gates], axis=1), preferred_element_type=F32)
          for h in range(N_HEADS)]
    outs = []
    for ci, (gu, _) in enumerate(gates):
        z = jnp.concatenate([zs[h][:, ci * LANES:(ci + 1) * LANES] for h in range(N_HEADS)], axis=1)
        outs.append(gu * (z + bias_ref[...]))
    return outs


def _rope(x, cos, sin):
    x1, x2 = x[:, :LANES], x[:, LANES:]
    return jnp.concatenate([x1 * cos - x2 * sin, x1 * sin + x2 * cos], axis=1)


def _retention_scores(q, k, cos, sin, hm_ref, decay_ref, zeta_ref):
    qb = _rope(q, cos, sin).astype(BF16)
    kr = _rope(k, cos, sin) * (RET_QK ** -0.5)
    k_heads = jnp.concatenate([(kr * hm_ref[h:h + 1, :]).astype(BF16) for h in range(N_HEADS)], axis=0)
    s = lax.dot_general(qb, k_heads, (((1,), (1,)), ((), ())), preferred_element_type=F32)
    kz_t = jnp.transpose(kr * zeta_ref[...]).astype(BF16)
    return qb, (s * decay_ref[...]).astype(BF16), kz_t


def _retention_mix(qb, sb, kz_t, v, g, state_scr, xi_ref, cd_ref, bd_ref, ng_ref):
    y_intra = jnp.concatenate(
        [jnp.dot(sb[:, h * LANES:(h + 1) * LANES], v[:, h * LANES:(h + 1) * LANES], preferred_element_type=F32)
         for h in range(N_HEADS)], axis=1)
    state = state_scr[...]
    y_cross = jnp.dot(qb, state.astype(BF16), preferred_element_type=F32) * xi_ref[...]
    upd = jnp.dot(kz_t, v, preferred_element_type=F32)
    state_scr[...] = state * cd_ref[...] + upd * bd_ref[...]
    y = y_intra + y_cross
    yn = jnp.concatenate([_head_norm(y[:, h * LANES:(h + 1) * LANES]) for h in range(N_HEADS)], axis=1)
    return jax.nn.silu(g) * (yn * ng_ref[...])


def _inmix_kernel(h_ref, g_ref, w_ref, wf_ref, pool_w_ref, pool_s_ref, sgu_g_ref, sgu_w_ref, sgu_b_ref,
                  cos_ref, sin_ref, hm_ref, zeta_ref, decay_ref, xi_ref, cd_ref, bd_ref, ret_g_ref,
                  qkv_ref, flog_ref, ya_ref, yb_ref, yd_ref, state_scr, halo_scr, *, tiles_per_seq):
    tm = h_ref.shape[0]
    tile_in_seq = pl.program_id(0) % tiles_per_seq

    @pl.when(tile_in_seq == 0)
    def _():
        state_scr[...] = jnp.zeros_like(state_scr)
        halo_scr[...] = jnp.zeros_like(halo_scr)

    x = h_ref[...]
    xn = (x * _rms_scale(x) * g_ref[...]).astype(BF16)

    def project(piece):
        cols = slice(piece * GROUP_WIDTH, (piece + 1) * GROUP_WIDTH)
        return jnp.dot(xn, w_ref[:, cols], preferred_element_type=F32)

    def emit_qkv(piece):
        qkv_ref[:, piece * GROUP_WIDTH:(piece + 1) * GROUP_WIDTH] = project(3 + piece).astype(BF16)

    chunks = [slice(ci * CHUNK, (ci + 1) * CHUNK) for ci in range(tm // CHUNK)]
    assert len(chunks) == 4

    a = project(0)
    pool_ops = _pool_windows(a, halo_scr[...], tile_in_seq * tm)
    halo_scr[...] = a[tm - POOL_HALO:, :]
    qk = project(6)
    ret_ops = [_retention_scores(qk[rows, :2 * LANES], qk[rows, 2 * LANES:], cos_ref[rows, :], sin_ref[rows, :],
                                 hm_ref, decay_ref, zeta_ref) for rows in chunks]
    rv, rg = project(7).astype(BF16), project(8)
    ya_ref[...] = _pool_mix(pool_ops, pool_w_ref, pool_s_ref).astype(BF16)

    def retention(ci):
        rows = chunks[ci]
        yd_ref[rows, :] = _retention_mix(*ret_ops[ci], rv[rows], rg[rows], state_scr, xi_ref, cd_ref, bd_ref,
                                         ret_g_ref).astype(BF16)

    u = project(1)
    retention(0)
    v = project(2)
    retention(1)
    sgu_ops = [_sgu_gates(u[rows], v[rows], sgu_g_ref) for rows in chunks]
    emit_qkv(0)
    retention(2)
    emit_qkv(1)
    retention(3)
    emit_qkv(2)
    r_i = lax.broadcasted_iota(jnp.int32, (CHUNK, CHUNK), 0)
    c_i = lax.broadcasted_iota(jnp.int32, (CHUNK, CHUNK), 1)
    w_causal = [jnp.where(c_i <= r_i, sgu_w_ref[h], 0.0).astype(BF16) for h in range(N_HEADS)]
    for rows, y in zip(chunks, _sgu_mix(sgu_ops, w_causal, sgu_b_ref)):
        yb_ref[rows, :] = y.astype(BF16)

    flog_ref[...] = lax.dot_general(wf_ref[...].astype(BF16), xn, (((1,), (1,)), ((), ())),
                                    preferred_element_type=F32)


def _inmix(h, g, w_main, wf, layer, pool_w, pool_scale, sgu_g, sgu_w, sgu_bias, ret_tables, ret_g, seq, tm):
    n, d = h.shape
    assert seq % tm == 0 and tm % CHUNK == 0
    tiles_per_seq = seq // tm
    cos, sin, hm, zeta, decay, xi, cd, bd = ret_tables

    def layer_resident(shape):
        return pl.BlockSpec((None,) + shape, lambda i: (layer, 0, 0), pipeline_mode=pl.Buffered(1))

    def rows(width):
        return pl.BlockSpec((tm, width), lambda i: (i, 0))

    pos_rows = pl.BlockSpec((tm, LANES), lambda i: (i % tiles_per_seq, 0))
    mix_out = jax.ShapeDtypeStruct((n, GROUP_WIDTH), BF16)
    return pl.pallas_call(
        functools.partial(_inmix_kernel, tiles_per_seq=tiles_per_seq),
        grid=(n // tm,),
        in_specs=[rows(d), _resident((1, d)), layer_resident(w_main.shape[1:]), layer_resident(wf.shape[1:]),
                  _resident(pool_w.shape), _resident((1, GROUP_WIDTH)),
                  _resident((1, GROUP_WIDTH)), _resident(sgu_w.shape), _resident((CHUNK, GROUP_WIDTH)),
                  pos_rows, pos_rows, _resident(hm.shape), _resident(zeta.shape), _resident(decay.shape),
                  _resident(xi.shape), _resident(cd.shape), _resident(bd.shape), _resident((1, GROUP_WIDTH))],
        out_specs=[rows(3 * GROUP_WIDTH), pl.BlockSpec((8, tm), lambda i: (0, i)),
                   rows(GROUP_WIDTH), rows(GROUP_WIDTH), rows(GROUP_WIDTH)],
        out_shape=[jax.ShapeDtypeStruct((n, 3 * GROUP_WIDTH), BF16), jax.ShapeDtypeStruct((8, n), F32),
                   mix_out, mix_out, mix_out],
        scratch_shapes=[pltpu.VMEM((2 * LANES, GROUP_WIDTH), F32), pltpu.VMEM((POOL_HALO, GROUP_WIDTH), F32)],
        compiler_params=_cparams("arbitrary"),
        name="inmix",
    )(h, g, w_main, wf, pool_w, pool_scale, sgu_g, sgu_w, sgu_bias, cos, sin, hm, zeta, decay, xi, cd, bd, ret_g)


def _fox_kernel(flog_ref, bf_ref, q_ref, k_ref, v_ref, *rest, tq, tk, td, hps, scale, n_cast):
    cast_src, o_ref, cast_dst, c_scr = rest[:n_cast], rest[n_cast], rest[n_cast + 1:2 * n_cast + 1], rest[-1]
    hg = pl.program_id(1)
    qi = pl.program_id(2)
    s_len = k_ref.shape[0]
    log2e = float(np.log2(np.e))

    for src, dst in zip(cast_src, cast_dst):
        dst[...] = src[...].astype(BF16)

    @pl.when((hg == 0) & (qi == 0))
    def _():
        x = flog_ref[...] + bf_ref[:, 0:1]
        c = jnp.minimum(x, 0.0) - jnp.log1p(jnp.exp(-jnp.abs(x)))
        lane = lax.broadcasted_iota(jnp.int32, c.shape, 1)
        sh = 1
        while sh < s_len:
            c = c + jnp.where(lane >= sh, pltpu.roll(c, sh, axis=1), 0.0)
            sh *= 2
        c = c * log2e
        for j in range(s_len // td):
            c_scr[j] = c[:, j * td:(j + 1) * td]

    heads = [slice(h * LANES, (h + 1) * LANES) for h in range(hps)]
    qs = [(q_ref[:, hs].astype(F32) * (scale * log2e)).astype(BF16) for hs in heads]
    row0 = qi * tq

    def update(state, q, kb, vb, ck, diagonal):
        m, l, acc = state
        s = lax.dot_general(q, kb, (((1,), (1,)), ((), ())), preferred_element_type=F32) - ck
        if diagonal:
            n = q.shape[0]
            own = s[:, s.shape[1] - n:]
            r_i = lax.broadcasted_iota(jnp.int32, own.shape, 0)
            c_i = lax.broadcasted_iota(jnp.int32, own.shape, 1)
            own = jnp.where(c_i <= r_i, own, NEG_BIG)
            s = own if n == s.shape[1] else jnp.concatenate([s[:, :s.shape[1] - n], own], axis=1)
        m_new = jnp.maximum(m, jnp.max(s, axis=-1, keepdims=True))
        alpha = jnp.exp2(m - m_new)
        p = jnp.exp2(s - m_new)
        l = alpha * l + jnp.sum(p, axis=-1, keepdims=True)
        acc = alpha * acc + jnp.dot(p.astype(BF16), vb, preferred_element_type=F32)
        return m_new, l, acc

    def bias_row(h, key_start, width):
        first = key_start // td
        parts = [c_scr[first + i, pl.ds(hg * hps + h, 1), :] for i in range(width // td)]
        return parts[0] if len(parts) == 1 else jnp.concatenate(parts, axis=1)

    def full_block(j, carry):
        start = pl.multiple_of(j * tk, tk)
        return tuple(
            update(carry[h], qs[h], k_ref[pl.ds(start, tk), hs], v_ref[pl.ds(start, tk), hs], bias_row(h, start, tk),
                   False)
            for h, hs in enumerate(heads))

    carry = tuple((jnp.full((tq, 1), NEG_BIG, F32), jnp.zeros((tq, 1), F32), jnp.zeros((tq, LANES), F32))
                  for _ in heads)
    carry = lax.fori_loop(0, row0 // tk, full_block, carry)

    start = pl.multiple_of(row0, tq)
    for h, hs in enumerate(heads):
        m, l, acc = carry[h]
        for r in range(tq // td):
            rows = slice(r * td, (r + 1) * td)
            n_keys = (r + 1) * td
            _, l_r, acc_r = update((m[rows], l[rows], acc[rows]), qs[h][rows], k_ref[pl.ds(start, n_keys), hs],
                                   v_ref[pl.ds(start, n_keys), hs], bias_row(h, start, n_keys), True)
            o_ref[rows, hs] = (acc_r / l_r).astype(BF16)


def _fox(qkv, flog_t, bf_tab, batch, seq, tq, tk, td, hps, cast_weights, layer):
    assert tq % td == 0 and tk % td == 0 and tq % tk == 0
    nq = seq // tq
    width = hps * LANES
    per_piece = GROUP_WIDTH // width
    n_steps = batch * per_piece * nq

    def step(b, g, i):
        return (b * per_piece + g) * nq + i

    cast_in, cast_out, cast_shapes = [], [], []
    for w in cast_weights:
        _, rows, cols = w.shape
        slab = rows // n_steps
        assert slab * n_steps == rows and slab % 16 == 0, (w.shape, n_steps)
        cast_in.append(pl.BlockSpec((None, slab, cols), lambda b, g, i: (layer, step(b, g, i), 0)))
        cast_out.append(pl.BlockSpec((slab, cols), lambda b, g, i: (step(b, g, i), 0)))
        cast_shapes.append(jax.ShapeDtypeStruct((rows, cols), BF16))

    outs = pl.pallas_call(
        functools.partial(_fox_kernel, tq=tq, tk=tk, td=td, hps=hps, scale=LANES ** -0.5, n_cast=len(cast_weights)),
        grid=(batch, per_piece, nq),
        in_specs=[
            pl.BlockSpec((8, seq), lambda b, g, i: (0, b)),
            _resident((8, LANES)),
            pl.BlockSpec((tq, width), lambda b, g, i: (b * nq + i, g)),
            pl.BlockSpec((seq, width), lambda b, g, i: (b, per_piece + g)),
            pl.BlockSpec((seq, width), lambda b, g, i: (b, 2 * per_piece + g)),
        ] + cast_in,
        out_specs=[pl.BlockSpec((tq, width), lambda b, g, i: (b * nq + i, g))] + cast_out,
        out_shape=[jax.ShapeDtypeStruct((batch * seq, GROUP_WIDTH), BF16)] + cast_shapes,
        scratch_shapes=[pltpu.VMEM((seq // td, 8, td), F32)],
        compiler_params=_cparams("parallel", "arbitrary", "arbitrary"),
        name="fox",
    )(flog_t, bf_tab, qkv, qkv, qkv, *cast_weights)
    return outs[0], outs[1:]


def _ret_tables(seq):
    half = RET_QK // 2
    inv = np.exp(-(np.arange(half, dtype=np.float32) / half) * np.float32(np.log(ROPE_BASE))).astype(np.float32)
    ang = np.arange(seq, dtype=np.float32)[:, None] * inv[None, :]
    cos = np.tile(np.cos(ang).astype(np.float32), (1, N_HEADS))
    sin = np.tile(np.sin(ang).astype(np.float32), (1, N_HEADS))
    gamma = (1.0 - np.exp((-5.0 - np.arange(N_HEADS, dtype=np.float32)) * np.float32(np.log(2.0)))).astype(np.float32)
    log_gamma = np.log(gamma).astype(np.float32)
    l = np.arange(CHUNK, dtype=np.float32)
    diff = l[:, None] - l[None, :]
    decay = np.where(diff[None] >= 0, np.exp(np.maximum(diff, 0.0)[None] * log_gamma[:, None, None]), 0.0)
    decay = np.transpose(decay, (1, 0, 2)).reshape(CHUNK, N_HEADS * CHUNK).astype(np.float32)
    xi = np.repeat(np.exp((l + 1.0)[:, None] * log_gamma[None, :]), LANES, axis=1).astype(np.float32)
    lane_head = np.tile(np.repeat(np.arange(N_HEADS), half), 2)
    zeta = np.exp((CHUNK - 1.0 - l)[:, None] * log_gamma[lane_head][None, :]).astype(np.float32)
    hm = np.zeros((8, 2 * LANES), np.float32)
    for h in range(N_HEADS):
        hm[h] = lane_head == h
    col_head = np.repeat(np.arange(N_HEADS), LANES)
    cd = np.exp(CHUNK * log_gamma)[col_head][None, :].astype(np.float32)
    bd = (lane_head[:, None] == col_head[None, :]).astype(np.float32)
    return tuple(jnp.asarray(t) for t in (cos, sin, hm, zeta, decay, xi, cd, bd))


def _outproj_kernel(h_ref, ya_ref, yb_ref, yc_ref, yd_ref, w_ref, o_ref, mix_scr):
    for gi, y_ref in enumerate((ya_ref, yb_ref, yc_ref, yd_ref)):
        mix_scr[:, gi * GROUP_WIDTH:(gi + 1) * GROUP_WIDTH] = y_ref[...]
    mix = mix_scr[...]
    for c in range(o_ref.shape[1] // GROUP_WIDTH):
        cols = slice(c * GROUP_WIDTH, (c + 1) * GROUP_WIDTH)
        o_ref[:, cols] = h_ref[:, cols] + jnp.dot(mix, w_ref[:, cols], preferred_element_type=F32)


def _outproj(h, ys, w_out, tm):
    n, d = h.shape
    y_spec = pl.BlockSpec((tm, GROUP_WIDTH), lambda i: (i, 0))
    w_spec = _resident(w_out.shape)
    return pl.pallas_call(
        _outproj_kernel,
        grid=(n // tm,),
        in_specs=[pl.BlockSpec((tm, d), lambda i: (i, 0)), y_spec, y_spec, y_spec, y_spec, w_spec],
        out_specs=pl.BlockSpec((tm, d), lambda i: (i, 0)),
        out_shape=jax.ShapeDtypeStruct((n, d), F32),
        scratch_shapes=[pltpu.VMEM((tm, N_HEADS * GROUP_WIDTH), BF16)],
        compiler_params=_cparams("parallel"),
        name="outproj",
    )(h, *ys, w_out)


def _mlp_kernel(h_ref, g_ref, w1_ref, w2_ref, gf_ref, o_ref, hn_scr, *, final_norm, n_split):
    f = pl.program_id(1)
    sub = h_ref.shape[0] // n_split
    row_groups = [slice(r * sub, (r + 1) * sub) for r in range(n_split)]

    @pl.when(f == 0)
    def _():
        for rows in row_groups:
            x = h_ref[rows, :]
            hn_scr[rows, :] = (x * _rms_scale(x) * g_ref[...]).astype(BF16)
            o_ref[rows, :] = x

    for rows in row_groups:
        a = jnp.dot(hn_scr[rows, :], w1_ref[...], preferred_element_type=F32)
        a = jnp.square(jnp.maximum(a, 0.0)).astype(BF16)
        o_ref[rows, :] += jnp.dot(a, w2_ref[...], preferred_element_type=F32)

    if final_norm:
        @pl.when(f == pl.num_programs(1) - 1)
        def _():
            for rows in row_groups:
                y = o_ref[rows, :]
                o_ref[rows, :] = y * _rms_scale(y) * gf_ref[...]


def _mlp(h, g, w1, w2, g_final, tm, tf, final_norm):
    n, d = h.shape
    d_ff = w1.shape[1]
    return pl.pallas_call(
        functools.partial(_mlp_kernel, final_norm=final_norm, n_split=max(1, tm // 512)),
        grid=(n // tm, d_ff // tf),
        in_specs=[
            pl.BlockSpec((tm, d), lambda i, f: (i, 0)),
            _resident((1, d)),
            pl.BlockSpec((d, tf), lambda i, f: (0, f)),
            pl.BlockSpec((tf, d), lambda i, f: (f, 0)),
            _resident((1, d)),
        ],
        out_specs=pl.BlockSpec((tm, d), lambda i, f: (i, 0)),
        out_shape=jax.ShapeDtypeStruct((n, d), F32),
        scratch_shapes=[pltpu.VMEM((tm, d), BF16)],
        compiler_params=_cparams("parallel", "arbitrary"),
        name="mlp",
    )(h, g, w1, w2, g_final)


def kernel(x, norm_mix_g, w_in, fox_b_f, pool_w, pool_scale, sgu_norm_g, sgu_w_s, sgu_b, ret_norm_g, w_out,
           norm_mlp_g, w_ff1, w_ff2, norm_final_g):
    batch, seq, d = x.shape
    depth = w_in.shape[0]
    n = batch * seq
    tm_in = min(512, seq)
    tm_out = tm_mlp = min(1024, n)
    tf_mlp = 1024
    fox_tiles = (min(FOX_TQ, seq), min(FOX_TK, seq), min(FOX_TD, seq), FOX_HEADS_PER_STEP)
    tables = _ret_tables(seq)

    w_main, w_forget = _wprep(w_in)

    h = x.reshape(n, d)
    g_final = norm_final_g.reshape(1, d)
    for layer in range(depth):
        sgu_bias = jnp.repeat(jnp.transpose(sgu_b[layer]), LANES, axis=1)
        qkv, flog_t, y_a, y_b, y_d = _inmix(
            h, norm_mix_g[layer].reshape(1, d), w_main, w_forget, layer,
            pool_w[layer], pool_scale[layer].reshape(1, GROUP_WIDTH),
            sgu_norm_g[layer].reshape(1, GROUP_WIDTH), sgu_w_s[layer], sgu_bias,
            tables, ret_norm_g[layer].reshape(1, GROUP_WIDTH), seq, tm_in)
        bf_tab = jnp.zeros((8, LANES), F32).at[:N_HEADS].set(jnp.broadcast_to(fox_b_f[layer][:, None], (N_HEADS, LANES)))
        y_c, (w_out_b, w_ff1_b, w_ff2_b) = _fox(qkv, flog_t, bf_tab, batch, seq, *fox_tiles,
                                                cast_weights=(w_out, w_ff1, w_ff2), layer=layer)

        h = _outproj(h, (y_a, y_b, y_c, y_d), w_out_b, tm_out)
        h = _mlp(h, norm_mlp_g[layer].reshape(1, d), w_ff1_b, w_ff2_b, g_final, tm_mlp, tf_mlp,
                 final_norm=(layer == depth - 1))
    return h.reshape(batch, seq, d)
```

```python
import functools

import numpy as np
import jax
import jax.numpy as jnp
from jax import lax
from jax.experimental import pallas as pl
from jax.experimental.pallas import tpu as pltpu

F32 = jnp.float32
BF16 = jnp.bfloat16

EPS = 1e-6
NEG_BIG = -1e30
GROUP_WIDTH = 512
LANES = 128
N_HEADS = 4
POOL_WINDOWS = (2, 4, 8, 16)
CHUNK = 128
POOL_HALO = 16
RET_QK = 64
ROPE_BASE = 10000.0
N_MAIN = 9 * GROUP_WIDTH
F_COL0 = 6 * GROUP_WIDTH
W_PREP_COLS = 256
V7X_VMEM_LIMIT = 56 * 1024 * 1024
FOX_TILE = 512


def _cparams(*sem):
    return pltpu.CompilerParams(dimension_semantics=sem, vmem_limit_bytes=V7X_VMEM_LIMIT)


def _resident(shape):
    nd = len(shape)
    return pl.BlockSpec(shape, lambda *_: (0,) * nd, pipeline_mode=pl.Buffered(1))


def _rms_scale(x):
    return lax.rsqrt(jnp.mean(x * x, axis=-1, keepdims=True) + EPS)


def _wprep_kernel(w_ref, f_ref, w_out_ref, f_out_ref):
    t = pl.program_id(0)
    n_layers = w_ref.shape[1]
    half = RET_QK // 2

    def emit(row_groups):
        for layer in range(n_layers):
            for b, ranges in enumerate(row_groups):
                x = jnp.concatenate([w_ref[r0:r1, layer, :] for r0, r1 in ranges], axis=0)
                w_out_ref[layer, :, b * LANES:(b + 1) * LANES] = jnp.transpose(x).astype(BF16)

    n_blocks = W_PREP_COLS // LANES
    retention_qk = (t >= F_COL0 // W_PREP_COLS) & (t < (F_COL0 + 2 * N_HEADS * RET_QK) // W_PREP_COLS)

    @pl.when(jnp.logical_not(retention_qk))
    def _():
        emit([[(b * LANES, (b + 1) * LANES)] for b in range(n_blocks)])

    @pl.when(retention_qk)
    def _():
        emit([[(hd * RET_QK + hf * half, hd * RET_QK + (hf + 1) * half) for hd in range(N_HEADS)]
              for hf in range(n_blocks)])

    keep = lax.broadcasted_iota(jnp.int32, f_out_ref.shape[1:], 0) < N_HEADS
    for layer in range(n_layers):
        f_out_ref[layer] = jnp.where(keep, f_ref[:, layer, :], 0.0)


def _wprep(w_in):
    n_layers, d, _ = w_in.shape
    assert W_PREP_COLS == N_HEADS * RET_QK and F_COL0 % W_PREP_COLS == 0
    w_t = jnp.transpose(w_in, (2, 0, 1))

    def src_row(t):
        return t * W_PREP_COLS + jnp.where(t >= F_COL0 // W_PREP_COLS, N_HEADS, 0)

    return pl.pallas_call(
        _wprep_kernel,
        grid=(N_MAIN // W_PREP_COLS,),
        in_specs=[
            pl.BlockSpec((pl.Element(W_PREP_COLS), pl.Element(n_layers), pl.Element(d)), lambda t: (src_row(t), 0, 0)),
            pl.BlockSpec((pl.Element(8), pl.Element(n_layers), pl.Element(d)), lambda t: (F_COL0, 0, 0)),
        ],
        out_specs=[
            pl.BlockSpec((n_layers, d, W_PREP_COLS), lambda t: (0, 0, t)),
            pl.BlockSpec((n_layers, 8, d), lambda t: (0, 0, 0)),
        ],
        out_shape=[jax.ShapeDtypeStruct((n_layers, d, N_MAIN), BF16), jax.ShapeDtypeStruct((n_layers, 8, d), F32)],
        compiler_params=_cparams("arbitrary"),
        name="wprep",
    )(w_t, w_t)


def _head_norm(x):
    mu = jnp.mean(x, axis=-1, keepdims=True)
    d = x - mu
    var = jnp.mean(d * d, axis=-1, keepdims=True)
    return d * lax.rsqrt(var + EPS)


def _pool_windows(a, halo, pos0):
    tm = a.shape[0]
    ext = jnp.concatenate([halo, a], axis=0)
    pos = lax.broadcasted_iota(jnp.int32, (tm, LANES), 0) + pos0
    ps = []
    for gi, w in enumerate(POOL_WINDOWS):
        cols = slice(gi * LANES, (gi + 1) * LANES)
        tot = ext[:, cols]
        sh = 1
        while sh < w:
            tot = tot + pltpu.roll(tot, sh, axis=0)
            sh *= 2
        cnt = jnp.minimum(pos + 1, w).astype(F32)
        ps.append((tot[POOL_HALO:] / cnt - a[:, cols]).astype(BF16))
    return ps


def _pool_mix(ps, pw_ref, ps_ref):
    ys = [jnp.dot(p, pw_ref[gi].astype(BF16), preferred_element_type=F32) for gi, p in enumerate(ps)]
    return jnp.concatenate(ys, axis=1) * ps_ref[...]


def _sgu_gates(u, v, ng_ref):
    v = jax.nn.gelu(v)
    vns = [(_head_norm(v[:, h * LANES:(h + 1) * LANES]) * ng_ref[:, h * LANES:(h + 1) * LANES]).astype(BF16)
           for h in range(N_HEADS)]
    return jax.nn.gelu(u), vns


def _sgu_mix(gates, w_causal, bias_ref):
    zs = [jnp.dot(w_causal[h], jnp.concatenate([vns[h] for _, vns in gates], axis=1), preferred_element_type=F32)
          for h in range(N_HEADS)]
    outs = []
    for ci, (gu, _) in enumerate(gates):
        z = jnp.concatenate([zs[h][:, ci * LANES:(ci + 1) * LANES] for h in range(N_HEADS)], axis=1)
        outs.append(gu * (z + bias_ref[...]))
    return outs


def _rope(x, cos, sin):
    x1, x2 = x[:, :LANES], x[:, LANES:]
    return jnp.concatenate([x1 * cos - x2 * sin, x1 * sin + x2 * cos], axis=1)


def _retention_scores(q, k, cos, sin, hm_ref, decay_ref, zeta_ref):
    qb = _rope(q, cos, sin).astype(BF16)
    kr = _rope(k, cos, sin) * (RET_QK ** -0.5)
    k_heads = jnp.concatenate([(kr * hm_ref[h:h + 1, :]).astype(BF16) for h in range(N_HEADS)], axis=0)
    s = lax.dot_general(qb, k_heads, (((1,), (1,)), ((), ())), preferred_element_type=F32)
    kz_t = jnp.transpose(kr * zeta_ref[...]).astype(BF16)
    return qb, (s * decay_ref[...]).astype(BF16), kz_t


def _retention_mix(qb, sb, kz_t, v, g, state_scr, xi_ref, cd_ref, bd_ref, ng_ref):
    y_intra = jnp.concatenate(
        [jnp.dot(sb[:, h * LANES:(h + 1) * LANES], v[:, h * LANES:(h + 1) * LANES], preferred_element_type=F32)
         for h in range(N_HEADS)], axis=1)
    state = state_scr[...]
    y_cross = jnp.dot(qb, state.astype(BF16), preferred_element_type=F32) * xi_ref[...]
    upd = jnp.dot(kz_t, v, preferred_element_type=F32)
    state_scr[...] = state * cd_ref[...] + upd * bd_ref[...]
    y = y_intra + y_cross
    yn = jnp.concatenate([_head_norm(y[:, h * LANES:(h + 1) * LANES]) for h in range(N_HEADS)], axis=1)
    return jax.nn.silu(g) * (yn * ng_ref[...])


def _inmix_kernel(h_ref, g_ref, w_ref, wf_ref, pool_w_ref, pool_s_ref, sgu_g_ref, sgu_w_ref, sgu_b_ref,
                  cos_ref, sin_ref, hm_ref, zeta_ref, decay_ref, xi_ref, cd_ref, bd_ref, ret_g_ref,
                  *rest, tiles_per_seq, n_cast):
    cast_src, (qkv_ref, flog_ref, ya_ref, yb_ref, yd_ref) = rest[:n_cast], rest[n_cast:n_cast + 5]
    cast_dst, (state_scr, halo_scr) = rest[n_cast + 5:2 * n_cast + 5], rest[2 * n_cast + 5:]
    _cast_slabs(cast_src, cast_dst)
    tm = h_ref.shape[0]
    tile_in_seq = pl.program_id(0) % tiles_per_seq

    @pl.when(tile_in_seq == 0)
    def _():
        state_scr[...] = jnp.zeros_like(state_scr)
        halo_scr[...] = jnp.zeros_like(halo_scr)

    x = h_ref[...]
    xn = (x * _rms_scale(x) * g_ref[...]).astype(BF16)

    def project(piece):
        cols = slice(piece * GROUP_WIDTH, (piece + 1) * GROUP_WIDTH)
        return jnp.dot(xn, w_ref[:, cols], preferred_element_type=F32)

    def emit_qkv(piece):
        qkv_ref[:, piece * GROUP_WIDTH:(piece + 1) * GROUP_WIDTH] = project(3 + piece).astype(BF16)

    chunks = [slice(ci * CHUNK, (ci + 1) * CHUNK) for ci in range(tm // CHUNK)]
    assert len(chunks) == 4

    a = project(0)
    pool_ops = _pool_windows(a, halo_scr[...], tile_in_seq * tm)
    halo_scr[...] = a[tm - POOL_HALO:, :]
    qk = project(6)
    ret_ops = [_retention_scores(qk[rows, :2 * LANES], qk[rows, 2 * LANES:], cos_ref[rows, :], sin_ref[rows, :],
                                 hm_ref, decay_ref, zeta_ref) for rows in chunks]
    rv, rg = project(7).astype(BF16), project(8)
    ya_ref[...] = _pool_mix(pool_ops, pool_w_ref, pool_s_ref).astype(BF16)

    def retention(ci):
        rows = chunks[ci]
        yd_ref[rows, :] = _retention_mix(*ret_ops[ci], rv[rows], rg[rows], state_scr, xi_ref, cd_ref, bd_ref,
                                         ret_g_ref).astype(BF16)

    u = project(1)
    retention(0)
    v = project(2)
    retention(1)
    sgu_ops = [_sgu_gates(u[rows], v[rows], sgu_g_ref) for rows in chunks]
    emit_qkv(0)
    retention(2)
    emit_qkv(1)
    retention(3)
    emit_qkv(2)
    r_i = lax.broadcasted_iota(jnp.int32, (CHUNK, CHUNK), 0)
    c_i = lax.broadcasted_iota(jnp.int32, (CHUNK, CHUNK), 1)
    w_causal = [jnp.where(c_i <= r_i, sgu_w_ref[h], 0.0).astype(BF16) for h in range(N_HEADS)]
    for rows, y in zip(chunks, _sgu_mix(sgu_ops, w_causal, sgu_b_ref)):
        yb_ref[rows, :] = y.astype(BF16)

    flog_ref[...] = lax.dot_general(wf_ref[...].astype(BF16), xn, (((1,), (1,)), ((), ())),
                                    preferred_element_type=F32)


def _inmix(h, g, w_main, wf, layer, pool_w, pool_scale, sgu_g, sgu_w, sgu_bias, ret_tables, ret_g, seq, tm,
           cast_weights):
    n, d = h.shape
    assert seq % tm == 0 and tm % CHUNK == 0
    tiles_per_seq = seq // tm
    cos, sin, hm, zeta, decay, xi, cd, bd = ret_tables

    def layer_resident(shape):
        return pl.BlockSpec((None,) + shape, lambda i: (layer, 0, 0), pipeline_mode=pl.Buffered(1))

    def rows(width):
        return pl.BlockSpec((tm, width), lambda i: (i, 0))

    pos_rows = pl.BlockSpec((tm, LANES), lambda i: (i % tiles_per_seq, 0))
    mix_out = jax.ShapeDtypeStruct((n, GROUP_WIDTH), BF16)
    cast_in, cast_out, cast_shapes = _cast_specs(cast_weights, layer, n // tm, lambda i: i)
    outs = pl.pallas_call(
        functools.partial(_inmix_kernel, tiles_per_seq=tiles_per_seq, n_cast=len(cast_weights)),
        grid=(n // tm,),
        in_specs=[rows(d), _resident((1, d)), layer_resident(w_main.shape[1:]), layer_resident(wf.shape[1:]),
                  _resident(pool_w.shape), _resident((1, GROUP_WIDTH)),
                  _resident((1, GROUP_WIDTH)), _resident(sgu_w.shape), _resident((CHUNK, GROUP_WIDTH)),
                  pos_rows, pos_rows, _resident(hm.shape), _resident(zeta.shape), _resident(decay.shape),
                  _resident(xi.shape), _resident(cd.shape), _resident(bd.shape), _resident((1, GROUP_WIDTH))]
        + cast_in,
        out_specs=[rows(3 * GROUP_WIDTH), pl.BlockSpec((8, tm), lambda i: (0, i)),
                   rows(GROUP_WIDTH), rows(GROUP_WIDTH), rows(GROUP_WIDTH)] + cast_out,
        out_shape=[jax.ShapeDtypeStruct((n, 3 * GROUP_WIDTH), BF16), jax.ShapeDtypeStruct((8, n), F32),
                   mix_out, mix_out, mix_out] + cast_shapes,
        scratch_shapes=[pltpu.VMEM((2 * LANES, GROUP_WIDTH), F32), pltpu.VMEM((POOL_HALO, GROUP_WIDTH), F32)],
        compiler_params=_cparams("arbitrary"),
        name="inmix",
    )(h, g, w_main, wf, pool_w, pool_scale, sgu_g, sgu_w, sgu_bias, cos, sin, hm, zeta, decay, xi, cd, bd, ret_g,
      *cast_weights)
    return outs[:5], outs[5:]


def _cast_specs(weights, layer, n_steps, step_of):
    specs_in, specs_out, shapes = [], [], []
    for w in weights:
        _, rows, cols = w.shape
        slab = rows // n_steps
        assert slab * n_steps == rows and slab % 16 == 0, (w.shape, n_steps)
        specs_in.append(pl.BlockSpec((None, slab, cols), lambda *idx: (layer, step_of(*idx), 0)))
        specs_out.append(pl.BlockSpec((slab, cols), lambda *idx: (step_of(*idx), 0)))
        shapes.append(jax.ShapeDtypeStruct((rows, cols), BF16))
    return specs_in, specs_out, shapes


def _cast_slabs(srcs, dsts):
    for src, dst in zip(srcs, dsts):
        dst[...] = src[...].astype(BF16)


def _foxout_kernel(flog_ref, bf_ref, q_ref, k_ref, v_ref, h_ref, ya_ref, yb_ref, yd_ref, w_ref, *rest,
                   tq, tk, scale, n_cast):
    cast_src, o_ref, cast_dst, c_scr = rest[:n_cast], rest[n_cast], rest[n_cast + 1:2 * n_cast + 1], rest[-1]
    qi = pl.program_id(1)
    s_len = k_ref.shape[0]
    log2e = float(np.log2(np.e))
    _cast_slabs(cast_src, cast_dst)

    @pl.when(qi == 0)
    def _():
        x = flog_ref[...] + bf_ref[:, 0:1]
        c = jnp.minimum(x, 0.0) - jnp.log1p(jnp.exp(-jnp.abs(x)))
        lane = lax.broadcasted_iota(jnp.int32, c.shape, 1)
        sh = 1
        while sh < s_len:
            c = c + jnp.where(lane >= sh, pltpu.roll(c, sh, axis=1), 0.0)
            sh *= 2
        c = c * log2e
        for j in range(s_len // tk):
            c_scr[j] = c[:, j * tk:(j + 1) * tk]

    heads = [slice(h * LANES, (h + 1) * LANES) for h in range(N_HEADS)]
    qs = [(q_ref[:, hs].astype(F32) * (scale * log2e)).astype(BF16) for hs in heads]

    def scores(h, hs, start, j):
        kb = k_ref[pl.ds(start, tk), hs]
        return lax.dot_general(qs[h], kb, (((1,), (1,)), ((), ())), preferred_element_type=F32) - c_scr[j, h:h + 1, :]

    def softmax_step(state, s):
        m, l, acc = state
        m_new = jnp.maximum(m, jnp.max(s, axis=-1, keepdims=True))
        alpha = jnp.exp2(m - m_new)
        p = jnp.exp2(s - m_new)
        return m_new, alpha * l + jnp.sum(p, axis=-1, keepdims=True), alpha * acc, p.astype(BF16)

    def full_block(j, carry):
        start = pl.multiple_of(j * tk, tk)
        out = []
        for h, hs in enumerate(heads):
            m, l, acc, p = softmax_step(carry[h], scores(h, hs, start, j))
            out.append((m, l, acc + jnp.dot(p, v_ref[pl.ds(start, tk), hs], preferred_element_type=F32)))
        return tuple(out)

    carry = tuple((jnp.full((tq, 1), NEG_BIG, F32), jnp.zeros((tq, 1), F32), jnp.zeros((tq, LANES), F32))
                  for _ in heads)
    carry = lax.fori_loop(0, qi, full_block, carry)

    start = pl.multiple_of(qi * tq, tq)
    r_i = lax.broadcasted_iota(jnp.int32, (tq, tk), 0)
    c_i = lax.broadcasted_iota(jnp.int32, (tq, tk), 1)
    diag_s = [jnp.where(c_i <= r_i, scores(h, hs, start, qi), NEG_BIG) for h, hs in enumerate(heads)]

    out_cols = [slice(c * GROUP_WIDTH, (c + 1) * GROUP_WIDTH) for c in range(o_ref.shape[1] // GROUP_WIDTH)]
    y_ab = jnp.concatenate([ya_ref[...], yb_ref[...]], axis=1)
    for cols in out_cols:
        o_ref[:, cols] = h_ref[:, cols] + jnp.dot(y_ab, w_ref[:2 * GROUP_WIDTH, cols], preferred_element_type=F32)

    y_cd = []
    for h, hs in enumerate(heads):
        _, l, acc, p = softmax_step(carry[h], diag_s[h])
        acc = acc + jnp.dot(p, v_ref[pl.ds(start, tk), hs], preferred_element_type=F32)
        y_cd.append((acc / l).astype(BF16))
    y_cd = jnp.concatenate(y_cd + [yd_ref[...]], axis=1)
    for cols in out_cols:
        o_ref[:, cols] += jnp.dot(y_cd, w_ref[2 * GROUP_WIDTH:, cols], preferred_element_type=F32)


def _foxout(qkv, flog_t, bf_tab, h, y_a, y_b, y_d, w_out, batch, seq, tile, cast_weights, layer):
    n, d = h.shape
    nq = seq // tile
    n_steps = batch * nq
    cast_in, cast_out, cast_shapes = _cast_specs(cast_weights, layer, n_steps, lambda b, i: b * nq + i)

    def tile_rows(width, col_block=0):
        return pl.BlockSpec((tile, width), lambda b, i: (b * nq + i, col_block))

    outs = pl.pallas_call(
        functools.partial(_foxout_kernel, tq=tile, tk=tile, scale=LANES ** -0.5, n_cast=len(cast_weights)),
        grid=(batch, nq),
        in_specs=[
            pl.BlockSpec((8, seq), lambda b, i: (0, b)),
            _resident((8, LANES)),
            tile_rows(GROUP_WIDTH, 0),
            pl.BlockSpec((seq, GROUP_WIDTH), lambda b, i: (b, 1), pipeline_mode=pl.Buffered(1)),
            pl.BlockSpec((seq, GROUP_WIDTH), lambda b, i: (b, 2), pipeline_mode=pl.Buffered(1)),
            tile_rows(d), tile_rows(GROUP_WIDTH), tile_rows(GROUP_WIDTH), tile_rows(GROUP_WIDTH),
            _resident(w_out.shape),
        ] + cast_in,
        out_specs=[tile_rows(d)] + cast_out,
        out_shape=[jax.ShapeDtypeStruct((n, d), F32)] + cast_shapes,
        scratch_shapes=[pltpu.VMEM((seq // tile, 8, tile), F32)],
        compiler_params=_cparams("parallel", "arbitrary"),
        name="foxout",
    )(flog_t, bf_tab, qkv, qkv, qkv, h, y_a, y_b, y_d, w_out, *cast_weights)
    return outs[0], outs[1:]


def _ret_tables(seq):
    half = RET_QK // 2
    inv = np.exp(-(np.arange(half, dtype=np.float32) / half) * np.float32(np.log(ROPE_BASE))).astype(np.float32)
    ang = np.arange(seq, dtype=np.float32)[:, None] * inv[None, :]
    cos = np.tile(np.cos(ang).astype(np.float32), (1, N_HEADS))
    sin = np.tile(np.sin(ang).astype(np.float32), (1, N_HEADS))
    gamma = (1.0 - np.exp((-5.0 - np.arange(N_HEADS, dtype=np.float32)) * np.float32(np.log(2.0)))).astype(np.float32)
    log_gamma = np.log(gamma).astype(np.float32)
    l = np.arange(CHUNK, dtype=np.float32)
    diff = l[:, None] - l[None, :]
    decay = np.where(diff[None] >= 0, np.exp(np.maximum(diff, 0.0)[None] * log_gamma[:, None, None]), 0.0)
    decay = np.transpose(decay, (1, 0, 2)).reshape(CHUNK, N_HEADS * CHUNK).astype(np.float32)
    xi = np.repeat(np.exp((l + 1.0)[:, None] * log_gamma[None, :]), LANES, axis=1).astype(np.float32)
    lane_head = np.tile(np.repeat(np.arange(N_HEADS), half), 2)
    zeta = np.exp((CHUNK - 1.0 - l)[:, None] * log_gamma[lane_head][None, :]).astype(np.float32)
    hm = np.zeros((8, 2 * LANES), np.float32)
    for h in range(N_HEADS):
        hm[h] = lane_head == h
    col_head = np.repeat(np.arange(N_HEADS), LANES)
    cd = np.exp(CHUNK * log_gamma)[col_head][None, :].astype(np.float32)
    bd = (lane_head[:, None] == col_head[None, :]).astype(np.float32)
    return tuple(jnp.asarray(t) for t in (cos, sin, hm, zeta, decay, xi, cd, bd))


def _mlp_kernel(h_ref, g_ref, w1_ref, w2_ref, gf_ref, o_ref, hn_scr, *, final_norm, n_split):
    f = pl.program_id(1)
    sub = h_ref.shape[0] // n_split
    row_groups = [slice(r * sub, (r + 1) * sub) for r in range(n_split)]

    @pl.when(f == 0)
    def _():
        for rows in row_groups:
            x = h_ref[rows, :]
            hn_scr[rows, :] = (x * _rms_scale(x) * g_ref[...]).astype(BF16)
            o_ref[rows, :] = x

    for rows in row_groups:
        a = jnp.dot(hn_scr[rows, :], w1_ref[...], preferred_element_type=F32)
        a = jnp.square(jnp.maximum(a, 0.0)).astype(BF16)
        o_ref[rows, :] += jnp.dot(a, w2_ref[...], preferred_element_type=F32)

    if final_norm:
        @pl.when(f == pl.num_programs(1) - 1)
        def _():
            for rows in row_groups:
                y = o_ref[rows, :]
                o_ref[rows, :] = y * _rms_scale(y) * gf_ref[...]


def _mlp(h, g, w1, w2, g_final, tm, tf, final_norm):
    n, d = h.shape
    d_ff = w1.shape[1]
    return pl.pallas_call(
        functools.partial(_mlp_kernel, final_norm=final_norm, n_split=max(1, tm // 512)),
        grid=(n // tm, d_ff // tf),
        in_specs=[
            pl.BlockSpec((tm, d), lambda i, f: (i, 0)),
            _resident((1, d)),
            pl.BlockSpec((d, tf), lambda i, f: (0, f)),
            pl.BlockSpec((tf, d), lambda i, f: (f, 0)),
            _resident((1, d)),
        ],
        out_specs=pl.BlockSpec((tm, d), lambda i, f: (i, 0)),
        out_shape=jax.ShapeDtypeStruct((n, d), F32),
        scratch_shapes=[pltpu.VMEM((tm, d), BF16)],
        compiler_params=_cparams("parallel", "arbitrary"),
        name="mlp",
    )(h, g, w1, w2, g_final)


def kernel(x, norm_mix_g, w_in, fox_b_f, pool_w, pool_scale, sgu_norm_g, sgu_w_s, sgu_b, ret_norm_g, w_out,
           norm_mlp_g, w_ff1, w_ff2, norm_final_g):
    batch, seq, d = x.shape
    depth = w_in.shape[0]
    n = batch * seq
    tm_in = min(512, seq)
    fox_tile = min(FOX_TILE, seq)
    tm_mlp = min(1024, n)
    tf_mlp = 1024
    tables = _ret_tables(seq)

    w_main, w_forget = _wprep(w_in)

    h = x.reshape(n, d)
    g_final = norm_final_g.reshape(1, d)
    for layer in range(depth):
        sgu_bias = jnp.repeat(jnp.transpose(sgu_b[layer]), LANES, axis=1)
        (qkv, flog_t, y_a, y_b, y_d), (w_out_b, w_ff1_b) = _inmix(
            h, norm_mix_g[layer].reshape(1, d), w_main, w_forget, layer,
            pool_w[layer], pool_scale[layer].reshape(1, GROUP_WIDTH),
            sgu_norm_g[layer].reshape(1, GROUP_WIDTH), sgu_w_s[layer], sgu_bias,
            tables, ret_norm_g[layer].reshape(1, GROUP_WIDTH), seq, tm_in, cast_weights=(w_out, w_ff1))
        bf_tab = jnp.zeros((8, LANES), F32).at[:N_HEADS].set(jnp.broadcast_to(fox_b_f[layer][:, None], (N_HEADS, LANES)))
        h, (w_ff2_b,) = _foxout(qkv, flog_t, bf_tab, h, y_a, y_b, y_d, w_out_b, batch, seq, fox_tile,
                                cast_weights=(w_ff2,), layer=layer)
        h = _mlp(h, norm_mlp_g[layer].reshape(1, d), w_ff1_b, w_ff2_b, g_final, tm_mlp, tf_mlp,
                 final_norm=(layer == depth - 1))
    return h.reshape(batch, seq, d)
```

```python
import functools

import numpy as np
import jax
import jax.numpy as jnp
from jax import lax
from jax.experimental import pallas as pl
from jax.experimental.pallas import tpu as pltpu

F32 = jnp.float32
BF16 = jnp.bfloat16

EPS = 1e-6
NEG_BIG = -1e30
GROUP_WIDTH = 512
LANES = 128
N_HEADS = 4
POOL_WINDOWS = (2, 4, 8, 16)
CHUNK = 128
POOL_HALO = 16
RET_QK = 64
ROPE_BASE = 10000.0
N_MAIN = 9 * GROUP_WIDTH
F_COL0 = 6 * GROUP_WIDTH
W_PREP_COLS = 256
V7X_VMEM_LIMIT = 56 * 1024 * 1024
FOX_TQ, FOX_TK, FOX_TD, FOX_HEADS_PER_STEP = 512, 512, 512, 4


def _cparams(*sem):
    return pltpu.CompilerParams(dimension_semantics=sem, vmem_limit_bytes=V7X_VMEM_LIMIT)


def _resident(shape):
    nd = len(shape)
    return pl.BlockSpec(shape, lambda *_: (0,) * nd, pipeline_mode=pl.Buffered(1))


def _rms_scale(x):
    return lax.rsqrt(jnp.mean(x * x, axis=-1, keepdims=True) + EPS)


def _wprep_kernel(w_ref, f_ref, w_out_ref, f_out_ref):
    t = pl.program_id(0)
    n_layers = w_ref.shape[1]
    half = RET_QK // 2

    def emit(row_groups):
        for layer in range(n_layers):
            for b, ranges in enumerate(row_groups):
                x = jnp.concatenate([w_ref[r0:r1, layer, :] for r0, r1 in ranges], axis=0)
                w_out_ref[layer, :, b * LANES:(b + 1) * LANES] = jnp.transpose(x).astype(BF16)

    n_blocks = W_PREP_COLS // LANES
    retention_qk = (t >= F_COL0 // W_PREP_COLS) & (t < (F_COL0 + 2 * N_HEADS * RET_QK) // W_PREP_COLS)

    @pl.when(jnp.logical_not(retention_qk))
    def _():
        emit([[(b * LANES, (b + 1) * LANES)] for b in range(n_blocks)])

    @pl.when(retention_qk)
    def _():
        emit([[(hd * RET_QK + hf * half, hd * RET_QK + (hf + 1) * half) for hd in range(N_HEADS)]
              for hf in range(n_blocks)])

    keep = lax.broadcasted_iota(jnp.int32, f_out_ref.shape[1:], 0) < N_HEADS
    for layer in range(n_layers):
        f_out_ref[layer] = jnp.where(keep, f_ref[:, layer, :], 0.0)


def _wprep(w_in):
    n_layers, d, _ = w_in.shape
    assert W_PREP_COLS == N_HEADS * RET_QK and F_COL0 % W_PREP_COLS == 0
    w_t = jnp.transpose(w_in, (2, 0, 1))

    def src_row(t):
        return t * W_PREP_COLS + jnp.where(t >= F_COL0 // W_PREP_COLS, N_HEADS, 0)

    return pl.pallas_call(
        _wprep_kernel,
        grid=(N_MAIN // W_PREP_COLS,),
        in_specs=[
            pl.BlockSpec((pl.Element(W_PREP_COLS), pl.Element(n_layers), pl.Element(d)), lambda t: (src_row(t), 0, 0)),
            pl.BlockSpec((pl.Element(8), pl.Element(n_layers), pl.Element(d)), lambda t: (F_COL0, 0, 0)),
        ],
        out_specs=[
            pl.BlockSpec((n_layers, d, W_PREP_COLS), lambda t: (0, 0, t)),
            pl.BlockSpec((n_layers, 8, d), lambda t: (0, 0, 0)),
        ],
        out_shape=[jax.ShapeDtypeStruct((n_layers, d, N_MAIN), BF16), jax.ShapeDtypeStruct((n_layers, 8, d), F32)],
        compiler_params=_cparams("arbitrary"),
        name="wprep",
    )(w_t, w_t)


def _head_norm(x):
    mu = jnp.mean(x, axis=-1, keepdims=True)
    d = x - mu
    var = jnp.mean(d * d, axis=-1, keepdims=True)
    return d * lax.rsqrt(var + EPS)


def _pool_windows(a, halo, pos0):
    tm = a.shape[0]
    ext = jnp.concatenate([halo, a], axis=0)
    pos = lax.broadcasted_iota(jnp.int32, (tm, LANES), 0) + pos0
    ps = []
    for gi, w in enumerate(POOL_WINDOWS):
        cols = slice(gi * LANES, (gi + 1) * LANES)
        tot = ext[:, cols]
        sh = 1
        while sh < w:
            tot = tot + pltpu.roll(tot, sh, axis=0)
            sh *= 2
        cnt = jnp.minimum(pos + 1, w).astype(F32)
        ps.append((tot[POOL_HALO:] / cnt - a[:, cols]).astype(BF16))
    return ps


def _pool_mix(ps, pw_ref, ps_ref):
    ys = [jnp.dot(p, pw_ref[gi].astype(BF16), preferred_element_type=F32) for gi, p in enumerate(ps)]
    return jnp.concatenate(ys, axis=1) * ps_ref[...]


def _sgu_gates(u, v, ng_ref):
    v = jax.nn.gelu(v)
    vns = [(_head_norm(v[:, h * LANES:(h + 1) * LANES]) * ng_ref[:, h * LANES:(h + 1) * LANES]).astype(BF16)
           for h in range(N_HEADS)]
    return jax.nn.gelu(u), vns


def _sgu_mix(gates, w_causal, bias_ref):
    zs = [jnp.dot(w_causal[h], jnp.concatenate([vns[h] for _, vns in gates], axis=1), preferred_element_type=F32)
          for h in range(N_HEADS)]
    outs = []
    for ci, (gu, _) in enumerate(gates):
        z = jnp.concatenate([zs[h][:, ci * LANES:(ci + 1) * LANES] for h in range(N_HEADS)], axis=1)
        outs.append(gu * (z + bias_ref[...]))
    return outs


def _rope(x, cos, sin):
    x1, x2 = x[:, :LANES], x[:, LANES:]
    return jnp.concatenate([x1 * cos - x2 * sin, x1 * sin + x2 * cos], axis=1)


def _retention_scores(q, k, cos, sin, hm_ref, decay_ref, zeta_ref):
    qb = _rope(q, cos, sin).astype(BF16)
    kr = _rope(k, cos, sin) * (RET_QK ** -0.5)
    k_heads = jnp.concatenate([(kr * hm_ref[h:h + 1, :]).astype(BF16) for h in range(N_HEADS)], axis=0)
    s = lax.dot_general(qb, k_heads, (((1,), (1,)), ((), ())), preferred_element_type=F32)
    kz_t = jnp.transpose(kr * zeta_ref[...]).astype(BF16)
    return qb, (s * decay_ref[...]).astype(BF16), kz_t


def _retention_mix(qb, sb, kz_t, v, g, state_scr, xi_ref, cd_ref, bd_ref, ng_ref):
    y_intra = jnp.concatenate(
        [jnp.dot(sb[:, h * LANES:(h + 1) * LANES], v[:, h * LANES:(h + 1) * LANES], preferred_element_type=F32)
         for h in range(N_HEADS)], axis=1)
    state = state_scr[...]
    y_cross = jnp.dot(qb, state.astype(BF16), preferred_element_type=F32) * xi_ref[...]
    upd = jnp.dot(kz_t, v, preferred_element_type=F32)
    state_scr[...] = state * cd_ref[...] + upd * bd_ref[...]
    y = y_intra + y_cross
    yn = jnp.concatenate([_head_norm(y[:, h * LANES:(h + 1) * LANES]) for h in range(N_HEADS)], axis=1)
    return jax.nn.silu(g) * (yn * ng_ref[...])


def _inmix_kernel(h_ref, g_ref, w_ref, wf_ref, pool_w_ref, pool_s_ref, sgu_g_ref, sgu_w_ref, sgu_b_ref,
                  cos_ref, sin_ref, hm_ref, zeta_ref, decay_ref, xi_ref, cd_ref, bd_ref, ret_g_ref,
                  qkv_ref, flog_ref, ya_ref, yb_ref, yd_ref, state_scr, halo_scr, *, tiles_per_seq):
    tm = h_ref.shape[0]
    tile_in_seq = pl.program_id(0) % tiles_per_seq

    @pl.when(tile_in_seq == 0)
    def _():
        state_scr[...] = jnp.zeros_like(state_scr)
        halo_scr[...] = jnp.zeros_like(halo_scr)

    x = h_ref[...]
    xn = (x * _rms_scale(x) * g_ref[...]).astype(BF16)

    def project(piece):
        cols = slice(piece * GROUP_WIDTH, (piece + 1) * GROUP_WIDTH)
        return jnp.dot(xn, w_ref[:, cols], preferred_element_type=F32)

    def emit_qkv(piece):
        qkv_ref[:, piece * GROUP_WIDTH:(piece + 1) * GROUP_WIDTH] = project(3 + piece).astype(BF16)

    chunks = [slice(ci * CHUNK, (ci + 1) * CHUNK) for ci in range(tm // CHUNK)]
    assert len(chunks) == 4

    a = project(0)
    pool_ops = _pool_windows(a, halo_scr[...], tile_in_seq * tm)
    halo_scr[...] = a[tm - POOL_HALO:, :]
    qk = project(6)
    ret_ops = [_retention_scores(qk[rows, :2 * LANES], qk[rows, 2 * LANES:], cos_ref[rows, :], sin_ref[rows, :],
                                 hm_ref, decay_ref, zeta_ref) for rows in chunks]
    rv, rg = project(7).astype(BF16), project(8)
    ya_ref[...] = _pool_mix(pool_ops, pool_w_ref, pool_s_ref).astype(BF16)

    def retention(ci):
        rows = chunks[ci]
        yd_ref[rows, :] = _retention_mix(*ret_ops[ci], rv[rows], rg[rows], state_scr, xi_ref, cd_ref, bd_ref,
                                         ret_g_ref).astype(BF16)

    u = project(1)
    retention(0)
    v = project(2)
    retention(1)
    sgu_ops = [_sgu_gates(u[rows], v[rows], sgu_g_ref) for rows in chunks]
    emit_qkv(0)
    retention(2)
    emit_qkv(1)
    retention(3)
    emit_qkv(2)
    r_i = lax.broadcasted_iota(jnp.int32, (CHUNK, CHUNK), 0)
    c_i = lax.broadcasted_iota(jnp.int32, (CHUNK, CHUNK), 1)
    w_causal = [jnp.where(c_i <= r_i, sgu_w_ref[h], 0.0).astype(BF16) for h in range(N_HEADS)]
    for rows, y in zip(chunks, _sgu_mix(sgu_ops, w_causal, sgu_b_ref)):
        yb_ref[rows, :] = y.astype(BF16)

    flog_ref[...] = lax.dot_general(wf_ref[...].astype(BF16), xn, (((1,), (1,)), ((), ())),
                                    preferred_element_type=F32)


def _inmix(h, g, w_main, wf, layer, pool_w, pool_scale, sgu_g, sgu_w, sgu_bias, ret_tables, ret_g, seq, tm):
    n, d = h.shape
    assert seq % tm == 0 and tm % CHUNK == 0
    tiles_per_seq = seq // tm
    cos, sin, hm, zeta, decay, xi, cd, bd = ret_tables

    def layer_resident(shape):
        return pl.BlockSpec((None,) + shape, lambda i: (layer, 0, 0), pipeline_mode=pl.Buffered(1))

    def rows(width):
        return pl.BlockSpec((tm, width), lambda i: (i, 0))

    pos_rows = pl.BlockSpec((tm, LANES), lambda i: (i % tiles_per_seq, 0))
    mix_out = jax.ShapeDtypeStruct((n, GROUP_WIDTH), BF16)
    return pl.pallas_call(
        functools.partial(_inmix_kernel, tiles_per_seq=tiles_per_seq),
        grid=(n // tm,),
        in_specs=[rows(d), _resident((1, d)), layer_resident(w_main.shape[1:]), layer_resident(wf.shape[1:]),
                  _resident(pool_w.shape), _resident((1, GROUP_WIDTH)),
                  _resident((1, GROUP_WIDTH)), _resident(sgu_w.shape), _resident((CHUNK, GROUP_WIDTH)),
                  pos_rows, pos_rows, _resident(hm.shape), _resident(zeta.shape), _resident(decay.shape),
                  _resident(xi.shape), _resident(cd.shape), _resident(bd.shape), _resident((1, GROUP_WIDTH))],
        out_specs=[rows(3 * GROUP_WIDTH), pl.BlockSpec((8, tm), lambda i: (0, i)),
                   rows(GROUP_WIDTH), rows(GROUP_WIDTH), rows(GROUP_WIDTH)],
        out_shape=[jax.ShapeDtypeStruct((n, 3 * GROUP_WIDTH), BF16), jax.ShapeDtypeStruct((8, n), F32),
                   mix_out, mix_out, mix_out],
        scratch_shapes=[pltpu.VMEM((2 * LANES, GROUP_WIDTH), F32), pltpu.VMEM((POOL_HALO, GROUP_WIDTH), F32)],
        compiler_params=_cparams("arbitrary"),
        name="inmix",
    )(h, g, w_main, wf, pool_w, pool_scale, sgu_g, sgu_w, sgu_bias, cos, sin, hm, zeta, decay, xi, cd, bd, ret_g)


def _fox_kernel(flog_ref, bf_ref, q_ref, k_ref, v_ref, *rest, tq, tk, td, hps, scale, n_cast):
    cast_src, o_ref, cast_dst, c_scr = rest[:n_cast], rest[n_cast], rest[n_cast + 1:2 * n_cast + 1], rest[-1]
    hg = pl.program_id(1)
    qi = pl.program_id(2)
    s_len = k_ref.shape[0]
    log2e = float(np.log2(np.e))

    for src, dst in zip(cast_src, cast_dst):
        dst[...] = src[...].astype(BF16)

    @pl.when((hg == 0) & (qi == 0))
    def _():
        x = flog_ref[...] + bf_ref[:, 0:1]
        c = jnp.minimum(x, 0.0) - jnp.log1p(jnp.exp(-jnp.abs(x)))
        lane = lax.broadcasted_iota(jnp.int32, c.shape, 1)
        sh = 1
        while sh < s_len:
            c = c + jnp.where(lane >= sh, pltpu.roll(c, sh, axis=1), 0.0)
            sh *= 2
        c = c * log2e
        for j in range(s_len // td):
            c_scr[j] = c[:, j * td:(j + 1) * td]

    heads = [slice(h * LANES, (h + 1) * LANES) for h in range(hps)]
    qs = [(q_ref[:, hs].astype(F32) * (scale * log2e)).astype(BF16) for hs in heads]

    def update(state, q, kb, vb, ck, diagonal):
        m, l, acc = state
        s = lax.dot_general(q, kb, (((1,), (1,)), ((), ())), preferred_element_type=F32) - ck
        if diagonal:
            n = q.shape[0]
            own = s[:, s.shape[1] - n:]
            r_i = lax.broadcasted_iota(jnp.int32, own.shape, 0)
            c_i = lax.broadcasted_iota(jnp.int32, own.shape, 1)
            own = jnp.where(c_i <= r_i, own, NEG_BIG)
            s = own if n == s.shape[1] else jnp.concatenate([s[:, :s.shape[1] - n], own], axis=1)
        m_new = jnp.maximum(m, jnp.max(s, axis=-1, keepdims=True))
        alpha = jnp.exp2(m - m_new)
        p = jnp.exp2(s - m_new)
        l = alpha * l + jnp.sum(p, axis=-1, keepdims=True)
        acc = alpha * acc + jnp.dot(p.astype(BF16), vb, preferred_element_type=F32)
        return m_new, l, acc

    def bias_row(h, key_start, width):
        first = key_start // td
        parts = [c_scr[first + i, pl.ds(hg * hps + h, 1), :] for i in range(width // td)]
        return parts[0] if len(parts) == 1 else jnp.concatenate(parts, axis=1)

    def full_block(j, carry):
        start = j * tk
        return tuple(
            update(carry[h], qs[h], k_ref[pl.ds(start, tk), hs], v_ref[pl.ds(start, tk), hs], bias_row(h, start, tk),
                   False)
            for h, hs in enumerate(heads))

    def attend(tile):
        carry = tuple((jnp.full((tq, 1), NEG_BIG, F32), jnp.zeros((tq, 1), F32), jnp.zeros((tq, LANES), F32))
                      for _ in heads)
        for j in range(tile * tq // tk):
            carry = full_block(j, carry)
        start = tile * tq
        for h, hs in enumerate(heads):
            m, l, acc = carry[h]
            for r in range(tq // td):
                rows = slice(r * td, (r + 1) * td)
                n_keys = (r + 1) * td
                _, l_r, acc_r = update((m[rows], l[rows], acc[rows]), qs[h][rows], k_ref[pl.ds(start, n_keys), hs],
                                       v_ref[pl.ds(start, n_keys), hs], bias_row(h, start, n_keys), True)
                o_ref[rows, hs] = (acc_r / l_r).astype(BF16)

    for tile in range(s_len // tq):
        pl.when(qi == tile)(functools.partial(attend, tile))


def _fox(qkv, flog_t, bf_tab, batch, seq, tq, tk, td, hps, cast_weights, layer):
    assert tq % td == 0 and tk % td == 0 and tq % tk == 0
    nq = seq // tq
    width = hps * LANES
    per_piece = GROUP_WIDTH // width
    n_steps = batch * per_piece * nq

    def step(b, g, i):
        return (b * per_piece + g) * nq + i

    cast_in, cast_out, cast_shapes = [], [], []
    for w in cast_weights:
        _, rows, cols = w.shape
        slab = rows // n_steps
        assert slab * n_steps == rows and slab % 16 == 0, (w.shape, n_steps)
        cast_in.append(pl.BlockSpec((None, slab, cols), lambda b, g, i: (layer, step(b, g, i), 0)))
        cast_out.append(pl.BlockSpec((slab, cols), lambda b, g, i: (step(b, g, i), 0)))
        cast_shapes.append(jax.ShapeDtypeStruct((rows, cols), BF16))

    outs = pl.pallas_call(
        functools.partial(_fox_kernel, tq=tq, tk=tk, td=td, hps=hps, scale=LANES ** -0.5, n_cast=len(cast_weights)),
        grid=(batch, per_piece, nq),
        in_specs=[
            pl.BlockSpec((8, seq), lambda b, g, i: (0, b)),
            _resident((8, LANES)),
            pl.BlockSpec((tq, width), lambda b, g, i: (b * nq + i, g)),
            pl.BlockSpec((seq, width), lambda b, g, i: (b, per_piece + g)),
            pl.BlockSpec((seq, width), lambda b, g, i: (b, 2 * per_piece + g)),
        ] + cast_in,
        out_specs=[pl.BlockSpec((tq, width), lambda b, g, i: (b * nq + i, g))] + cast_out,
        out_shape=[jax.ShapeDtypeStruct((batch * seq, GROUP_WIDTH), BF16)] + cast_shapes,
        scratch_shapes=[pltpu.VMEM((seq // td, 8, td), F32)],
        compiler_params=_cparams("parallel", "arbitrary", "arbitrary"),
        name="fox",
    )(flog_t, bf_tab, qkv, qkv, qkv, *cast_weights)
    return outs[0], outs[1:]


def _ret_tables(seq):
    half = RET_QK // 2
    inv = np.exp(-(np.arange(half, dtype=np.float32) / half) * np.float32(np.log(ROPE_BASE))).astype(np.float32)
    ang = np.arange(seq, dtype=np.float32)[:, None] * inv[None, :]
    cos = np.tile(np.cos(ang).astype(np.float32), (1, N_HEADS))
    sin = np.tile(np.sin(ang).astype(np.float32), (1, N_HEADS))
    gamma = (1.0 - np.exp((-5.0 - np.arange(N_HEADS, dtype=np.float32)) * np.float32(np.log(2.0)))).astype(np.float32)
    log_gamma = np.log(gamma).astype(np.float32)
    l = np.arange(CHUNK, dtype=np.float32)
    diff = l[:, None] - l[None, :]
    decay = np.where(diff[None] >= 0, np.exp(np.maximum(diff, 0.0)[None] * log_gamma[:, None, None]), 0.0)
    decay = np.transpose(decay, (1, 0, 2)).reshape(CHUNK, N_HEADS * CHUNK).astype(np.float32)
    xi = np.repeat(np.exp((l + 1.0)[:, None] * log_gamma[None, :]), LANES, axis=1).astype(np.float32)
    lane_head = np.tile(np.repeat(np.arange(N_HEADS), half), 2)
    zeta = np.exp((CHUNK - 1.0 - l)[:, None] * log_gamma[lane_head][None, :]).astype(np.float32)
    hm = np.zeros((8, 2 * LANES), np.float32)
    for h in range(N_HEADS):
        hm[h] = lane_head == h
    col_head = np.repeat(np.arange(N_HEADS), LANES)
    cd = np.exp(CHUNK * log_gamma)[col_head][None, :].astype(np.float32)
    bd = (lane_head[:, None] == col_head[None, :]).astype(np.float32)
    return tuple(jnp.asarray(t) for t in (cos, sin, hm, zeta, decay, xi, cd, bd))


def _outproj_kernel(h_ref, ya_ref, yb_ref, yc_ref, yd_ref, w_ref, o_ref, mix_scr):
    for gi, y_ref in enumerate((ya_ref, yb_ref, yc_ref, yd_ref)):
        mix_scr[:, gi * GROUP_WIDTH:(gi + 1) * GROUP_WIDTH] = y_ref[...]
    mix = mix_scr[...]
    for c in range(o_ref.shape[1] // GROUP_WIDTH):
        cols = slice(c * GROUP_WIDTH, (c + 1) * GROUP_WIDTH)
        o_ref[:, cols] = h_ref[:, cols] + jnp.dot(mix, w_ref[:, cols], preferred_element_type=F32)


def _outproj(h, ys, w_out, tm):
    n, d = h.shape
    y_spec = pl.BlockSpec((tm, GROUP_WIDTH), lambda i: (i, 0))
    w_spec = _resident(w_out.shape)
    return pl.pallas_call(
        _outproj_kernel,
        grid=(n // tm,),
        in_specs=[pl.BlockSpec((tm, d), lambda i: (i, 0)), y_spec, y_spec, y_spec, y_spec, w_spec],
        out_specs=pl.BlockSpec((tm, d), lambda i: (i, 0)),
        out_shape=jax.ShapeDtypeStruct((n, d), F32),
        scratch_shapes=[pltpu.VMEM((tm, N_HEADS * GROUP_WIDTH), BF16)],
        compiler_params=_cparams("parallel"),
        name="outproj",
    )(h, *ys, w_out)


def _mlp_kernel(h_ref, g_ref, w1_ref, w2_ref, gf_ref, o_ref, hn_scr, *, final_norm, n_split):
    f = pl.program_id(1)
    sub = h_ref.shape[0] // n_split
    row_groups = [slice(r * sub, (r + 1) * sub) for r in range(n_split)]

    @pl.when(f == 0)
    def _():
        for rows in row_groups:
            x = h_ref[rows, :]
            hn_scr[rows, :] = (x * _rms_scale(x) * g_ref[...]).astype(BF16)
            o_ref[rows, :] = x

    for rows in row_groups:
        a = jnp.dot(hn_scr[rows, :], w1_ref[...], preferred_element_type=F32)
        a = jnp.square(jnp.maximum(a, 0.0)).astype(BF16)
        o_ref[rows, :] += jnp.dot(a, w2_ref[...], preferred_element_type=F32)

    if final_norm:
        @pl.when(f == pl.num_programs(1) - 1)
        def _():
            for rows in row_groups:
                y = o_ref[rows, :]
                o_ref[rows, :] = y * _rms_scale(y) * gf_ref[...]


def _mlp(h, g, w1, w2, g_final, tm, tf, final_norm):
    n, d = h.shape
    d_ff = w1.shape[1]
    return pl.pallas_call(
        functools.partial(_mlp_kernel, final_norm=final_norm, n_split=max(1, tm // 512)),
        grid=(n // tm, d_ff // tf),
        in_specs=[
            pl.BlockSpec((tm, d), lambda i, f: (i, 0)),
            _resident((1, d)),
            pl.BlockSpec((d, tf), lambda i, f: (0, f)),
            pl.BlockSpec((tf, d), lambda i, f: (f, 0)),
            _resident((1, d)),
        ],
        out_specs=pl.BlockSpec((tm, d), lambda i, f: (i, 0)),
        out_shape=jax.ShapeDtypeStruct((n, d), F32),
        scratch_shapes=[pltpu.VMEM((tm, d), BF16)],
        compiler_params=_cparams("parallel", "arbitrary"),
        name="mlp",
    )(h, g, w1, w2, g_final)


def kernel(x, norm_mix_g, w_in, fox_b_f, pool_w, pool_scale, sgu_norm_g, sgu_w_s, sgu_b, ret_norm_g, w_out,
           norm_mlp_g, w_ff1, w_ff2, norm_final_g):
    batch, seq, d = x.shape
    depth = w_in.shape[0]
    n = batch * seq
    tm_in = min(512, seq)
    tm_out = tm_mlp = min(1024, n)
    tf_mlp = 1024
    fox_tiles = (min(FOX_TQ, seq), min(FOX_TK, seq), min(FOX_TD, seq), FOX_HEADS_PER_STEP)
    tables = _ret_tables(seq)

    w_main, w_forget = _wprep(w_in)

    h = x.reshape(n, d)
    g_final = norm_final_g.reshape(1, d)
    for layer in range(depth):
        sgu_bias = jnp.repeat(jnp.transpose(sgu_b[layer]), LANES, axis=1)
        qkv, flog_t, y_a, y_b, y_d = _inmix(
            h, norm_mix_g[layer].reshape(1, d), w_main, w_forget, layer,
            pool_w[layer], pool_scale[layer].reshape(1, GROUP_WIDTH),
            sgu_norm_g[layer].reshape(1, GROUP_WIDTH), sgu_w_s[layer], sgu_bias,
            tables, ret_norm_g[layer].reshape(1, GROUP_WIDTH), seq, tm_in)
        bf_tab = jnp.zeros((8, LANES), F32).at[:N_HEADS].set(jnp.broadcast_to(fox_b_f[layer][:, None], (N_HEADS, LANES)))
        y_c, (w_out_b, w_ff1_b, w_ff2_b) = _fox(qkv, flog_t, bf_tab, batch, seq, *fox_tiles,
                                                cast_weights=(w_out, w_ff1, w_ff2), layer=layer)

        h = _outproj(h, (y_a, y_b, y_c, y_d), w_out_b, tm_out)
        h = _mlp(h, norm_mlp_g[layer].reshape(1, d), w_ff1_b, w_ff2_b, g_final, tm_mlp, tf_mlp,
                 final_norm=(layer == depth - 1))
    return h.reshape(batch, seq, d)
```

```python
import functools

import numpy as np
import jax
import jax.numpy as jnp
from jax import lax
from jax.experimental import pallas as pl
from jax.experimental.pallas import tpu as pltpu

F32 = jnp.float32
BF16 = jnp.bfloat16

EPS = 1e-6
NEG_BIG = -1e30
GROUP_WIDTH = 512
LANES = 128
SUBLANES = 8
BF16_TILE_ROWS = 16
ROW_TILE = 512
WIDE_ROW_TILE = 1024
FF_TILE = 1024
N_HEADS = 4
POOL_WINDOWS = (2, 4, 8, 16)
CHUNK = 128
POOL_HALO = 16
RET_QK = 64
ROPE_BASE = 10000.0
N_MAIN = 9 * GROUP_WIDTH
F_COL0 = 6 * GROUP_WIDTH
W_PREP_COLS = 256
V7X_VMEM_LIMIT = 56 * 1024 * 1024
FOX_TQ, FOX_TK, FOX_TD, FOX_HEADS_PER_STEP = ROW_TILE, ROW_TILE, ROW_TILE, N_HEADS


def _cparams(*sem):
    return pltpu.CompilerParams(dimension_semantics=sem, vmem_limit_bytes=V7X_VMEM_LIMIT)


def _resident(shape):
    nd = len(shape)
    return pl.BlockSpec(shape, lambda *_: (0,) * nd, pipeline_mode=pl.Buffered(1))


def _rms_scale(x):
    return lax.rsqrt(jnp.mean(x * x, axis=-1, keepdims=True) + EPS)


def _wprep_kernel(w_ref, f_ref, w_out_ref, f_out_ref):
    t = pl.program_id(0)
    n_layers = w_ref.shape[1]
    half = RET_QK // 2

    def emit(row_groups):
        for layer in range(n_layers):
            for b, ranges in enumerate(row_groups):
                x = jnp.concatenate([w_ref[r0:r1, layer, :] for r0, r1 in ranges], axis=0)
                w_out_ref[layer, :, b * LANES:(b + 1) * LANES] = jnp.transpose(x).astype(BF16)

    n_blocks = W_PREP_COLS // LANES
    retention_qk = (t >= F_COL0 // W_PREP_COLS) & (t < (F_COL0 + 2 * N_HEADS * RET_QK) // W_PREP_COLS)

    @pl.when(jnp.logical_not(retention_qk))
    def _():
        emit([[(b * LANES, (b + 1) * LANES)] for b in range(n_blocks)])

    @pl.when(retention_qk)
    def _():
        emit([[(hd * RET_QK + hf * half, hd * RET_QK + (hf + 1) * half) for hd in range(N_HEADS)]
              for hf in range(n_blocks)])

    keep = lax.broadcasted_iota(jnp.int32, f_out_ref.shape[1:], 0) < N_HEADS
    for layer in range(n_layers):
        f_out_ref[layer] = jnp.where(keep, f_ref[:, layer, :], 0.0)


def _wprep(w_in):
    n_layers, d, _ = w_in.shape
    assert W_PREP_COLS == N_HEADS * RET_QK and F_COL0 % W_PREP_COLS == 0
    w_t = jnp.transpose(w_in, (2, 0, 1))

    def src_row(t):
        return t * W_PREP_COLS + jnp.where(t >= F_COL0 // W_PREP_COLS, N_HEADS, 0)

    return pl.pallas_call(
        _wprep_kernel,
        grid=(N_MAIN // W_PREP_COLS,),
        in_specs=[
            pl.BlockSpec((pl.Element(W_PREP_COLS), pl.Element(n_layers), pl.Element(d)), lambda t: (src_row(t), 0, 0)),
            pl.BlockSpec((pl.Element(SUBLANES), pl.Element(n_layers), pl.Element(d)), lambda t: (F_COL0, 0, 0)),
        ],
        out_specs=[
            pl.BlockSpec((n_layers, d, W_PREP_COLS), lambda t: (0, 0, t)),
            pl.BlockSpec((n_layers, SUBLANES, d), lambda t: (0, 0, 0)),
        ],
        out_shape=[jax.ShapeDtypeStruct((n_layers, d, N_MAIN), BF16),
                   jax.ShapeDtypeStruct((n_layers, SUBLANES, d), F32)],
        compiler_params=_cparams("arbitrary"),
        name="wprep",
    )(w_t, w_t)


def _head_norm(x):
    mu = jnp.mean(x, axis=-1, keepdims=True)
    d = x - mu
    var = jnp.mean(d * d, axis=-1, keepdims=True)
    return d * lax.rsqrt(var + EPS)


def _pool_windows(a, halo, pos0):
    tm = a.shape[0]
    ext = jnp.concatenate([halo, a], axis=0)
    pos = lax.broadcasted_iota(jnp.int32, (tm, LANES), 0) + pos0
    ps = []
    for gi, w in enumerate(POOL_WINDOWS):
        cols = slice(gi * LANES, (gi + 1) * LANES)
        tot = ext[:, cols]
        sh = 1
        while sh < w:
            tot = tot + pltpu.roll(tot, sh, axis=0)
            sh *= 2
        cnt = jnp.minimum(pos + 1, w).astype(F32)
        ps.append((tot[POOL_HALO:] / cnt - a[:, cols]).astype(BF16))
    return ps


def _pool_mix(ps, pw_ref, ps_ref):
    ys = [jnp.dot(p, pw_ref[gi].astype(BF16), preferred_element_type=F32) for gi, p in enumerate(ps)]
    return jnp.concatenate(ys, axis=1) * ps_ref[...]


def _sgu_gates(u, v, ng_ref):
    v = jax.nn.gelu(v)
    vns = [(_head_norm(v[:, h * LANES:(h + 1) * LANES]) * ng_ref[:, h * LANES:(h + 1) * LANES]).astype(BF16)
           for h in range(N_HEADS)]
    return jax.nn.gelu(u), vns


def _sgu_mix(gates, w_causal, bias_ref):
    zs = [jnp.dot(w_causal[h], jnp.concatenate([vns[h] for _, vns in gates], axis=1), preferred_element_type=F32)
          for h in range(N_HEADS)]
    outs = []
    for ci, (gu, _) in enumerate(gates):
        z = jnp.concatenate([zs[h][:, ci * LANES:(ci + 1) * LANES] for h in range(N_HEADS)], axis=1)
        outs.append(gu * (z + bias_ref[...]))
    return outs


def _rope(x, cos, sin):
    x1, x2 = x[:, :LANES], x[:, LANES:]
    return jnp.concatenate([x1 * cos - x2 * sin, x1 * sin + x2 * cos], axis=1)


def _retention_scores(q, k, cos, sin, hm_ref, decay_ref, zeta_ref):
    qb = _rope(q, cos, sin).astype(BF16)
    kr = _rope(k, cos, sin) * (RET_QK ** -0.5)
    k_heads = jnp.concatenate([(kr * hm_ref[h:h + 1, :]).astype(BF16) for h in range(N_HEADS)], axis=0)
    s = lax.dot_general(qb, k_heads, (((1,), (1,)), ((), ())), preferred_element_type=F32)
    kz_t = jnp.transpose(kr * zeta_ref[...]).astype(BF16)
    return qb, (s * decay_ref[...]).astype(BF16), kz_t


def _retention_mix(qb, sb, kz_t, v, g, state_scr, xi_ref, cd_ref, bd_ref, ng_ref):
    y_intra = jnp.concatenate(
        [jnp.dot(sb[:, h * LANES:(h + 1) * LANES], v[:, h * LANES:(h + 1) * LANES], preferred_element_type=F32)
         for h in range(N_HEADS)], axis=1)
    state = state_scr[...]
    y_cross = jnp.dot(qb, state.astype(BF16), preferred_element_type=F32) * xi_ref[...]
    upd = jnp.dot(kz_t, v, preferred_element_type=F32)
    state_scr[...] = state * cd_ref[...] + upd * bd_ref[...]
    y = y_intra + y_cross
    yn = jnp.concatenate([_head_norm(y[:, h * LANES:(h + 1) * LANES]) for h in range(N_HEADS)], axis=1)
    return jax.nn.silu(g) * (yn * ng_ref[...])


def _inmix_kernel(h_ref, g_ref, w_ref, wf_ref, pool_w_ref, pool_s_ref, sgu_g_ref, sgu_w_ref, sgu_b_ref,
                  cos_ref, sin_ref, hm_ref, zeta_ref, decay_ref, xi_ref, cd_ref, bd_ref, ret_g_ref,
                  qkv_ref, flog_ref, ya_ref, yb_ref, yd_ref, state_scr, halo_scr, *, tiles_per_seq):
    tm = h_ref.shape[0]
    tile_in_seq = pl.program_id(0) % tiles_per_seq

    @pl.when(tile_in_seq == 0)
    def _():
        state_scr[...] = jnp.zeros_like(state_scr)
        halo_scr[...] = jnp.zeros_like(halo_scr)

    x = h_ref[...]
    xn = (x * _rms_scale(x) * g_ref[...]).astype(BF16)

    def project(piece):
        cols = slice(piece * GROUP_WIDTH, (piece + 1) * GROUP_WIDTH)
        return jnp.dot(xn, w_ref[:, cols], preferred_element_type=F32)

    def emit_qkv(piece):
        qkv_ref[:, piece * GROUP_WIDTH:(piece + 1) * GROUP_WIDTH] = project(3 + piece).astype(BF16)

    chunks = [slice(ci * CHUNK, (ci + 1) * CHUNK) for ci in range(tm // CHUNK)]
    assert len(chunks) == 4

    a = project(0)
    pool_ops = _pool_windows(a, halo_scr[...], tile_in_seq * tm)
    halo_scr[...] = a[tm - POOL_HALO:, :]
    qk = project(6)
    ret_ops = [_retention_scores(qk[rows, :2 * LANES], qk[rows, 2 * LANES:], cos_ref[rows, :], sin_ref[rows, :],
                                 hm_ref, decay_ref, zeta_ref) for rows in chunks]
    rv, rg = project(7).astype(BF16), project(8)
    ya_ref[...] = _pool_mix(pool_ops, pool_w_ref, pool_s_ref).astype(BF16)

    def retention(ci):
        rows = chunks[ci]
        yd_ref[rows, :] = _retention_mix(*ret_ops[ci], rv[rows], rg[rows], state_scr, xi_ref, cd_ref, bd_ref,
                                         ret_g_ref).astype(BF16)

    u = project(1)
    retention(0)
    v = project(2)
    retention(1)
    sgu_ops = [_sgu_gates(u[rows], v[rows], sgu_g_ref) for rows in chunks]
    emit_qkv(0)
    retention(2)
    emit_qkv(1)
    retention(3)
    emit_qkv(2)
    r_i = lax.broadcasted_iota(jnp.int32, (CHUNK, CHUNK), 0)
    c_i = lax.broadcasted_iota(jnp.int32, (CHUNK, CHUNK), 1)
    w_causal = [jnp.where(c_i <= r_i, sgu_w_ref[h], 0.0).astype(BF16) for h in range(N_HEADS)]
    for rows, y in zip(chunks, _sgu_mix(sgu_ops, w_causal, sgu_b_ref)):
        yb_ref[rows, :] = y.astype(BF16)

    flog_ref[...] = lax.dot_general(wf_ref[...].astype(BF16), xn, (((1,), (1,)), ((), ())),
                                    preferred_element_type=F32)


def _inmix(h, g, w_main, wf, layer, pool_w, pool_scale, sgu_g, sgu_w, sgu_bias, ret_tables, ret_g, seq, tm):
    n, d = h.shape
    assert seq % tm == 0 and tm % CHUNK == 0
    tiles_per_seq = seq // tm
    cos, sin, hm, zeta, decay, xi, cd, bd = ret_tables

    def layer_resident(shape):
        return pl.BlockSpec((None,) + shape, lambda i: (layer, 0, 0), pipeline_mode=pl.Buffered(1))

    def rows(width):
        return pl.BlockSpec((tm, width), lambda i: (i, 0))

    pos_rows = pl.BlockSpec((tm, LANES), lambda i: (i % tiles_per_seq, 0))
    mix_out = jax.ShapeDtypeStruct((n, GROUP_WIDTH), BF16)
    return pl.pallas_call(
        functools.partial(_inmix_kernel, tiles_per_seq=tiles_per_seq),
        grid=(n // tm,),
        in_specs=[rows(d), _resident((1, d)), layer_resident(w_main.shape[1:]), layer_resident(wf.shape[1:]),
                  _resident(pool_w.shape), _resident((1, GROUP_WIDTH)),
                  _resident((1, GROUP_WIDTH)), _resident(sgu_w.shape), _resident((CHUNK, GROUP_WIDTH)),
                  pos_rows, pos_rows, _resident(hm.shape), _resident(zeta.shape), _resident(decay.shape),
                  _resident(xi.shape), _resident(cd.shape), _resident(bd.shape), _resident((1, GROUP_WIDTH))],
        out_specs=[rows(3 * GROUP_WIDTH), pl.BlockSpec((SUBLANES, tm), lambda i: (0, i)),
                   rows(GROUP_WIDTH), rows(GROUP_WIDTH), rows(GROUP_WIDTH)],
        out_shape=[jax.ShapeDtypeStruct((n, 3 * GROUP_WIDTH), BF16), jax.ShapeDtypeStruct((SUBLANES, n), F32),
                   mix_out, mix_out, mix_out],
        scratch_shapes=[pltpu.VMEM((2 * LANES, GROUP_WIDTH), F32), pltpu.VMEM((POOL_HALO, GROUP_WIDTH), F32)],
        compiler_params=_cparams("arbitrary"),
        name="inmix",
    )(h, g, w_main, wf, pool_w, pool_scale, sgu_g, sgu_w, sgu_bias, cos, sin, hm, zeta, decay, xi, cd, bd, ret_g)


def _fox_kernel(flog_ref, bf_ref, q_ref, k_ref, v_ref, *rest, tq, tk, td, hps, scale, n_cast):
    cast_src, o_ref, cast_dst, c_scr = rest[:n_cast], rest[n_cast], rest[n_cast + 1:2 * n_cast + 1], rest[-1]
    hg = pl.program_id(1)
    qi = pl.program_id(2)
    s_len = k_ref.shape[0]
    log2e = float(np.log2(np.e))

    for src, dst in zip(cast_src, cast_dst):
        dst[...] = src[...].astype(BF16)

    @pl.when((hg == 0) & (qi == 0))
    def _():
        x = flog_ref[...] + bf_ref[:, 0:1]
        c = jnp.minimum(x, 0.0) - jnp.log1p(jnp.exp(-jnp.abs(x)))
        lane = lax.broadcasted_iota(jnp.int32, c.shape, 1)
        sh = 1
        while sh < s_len:
            c = c + jnp.where(lane >= sh, pltpu.roll(c, sh, axis=1), 0.0)
            sh *= 2
        c = c * log2e
        for j in range(s_len // td):
            c_scr[j] = c[:, j * td:(j + 1) * td]

    heads = [slice(h * LANES, (h + 1) * LANES) for h in range(hps)]
    qs = [(q_ref[:, hs].astype(F32) * (scale * log2e)).astype(BF16) for hs in heads]

    def update(state, q, kb, vb, ck, diagonal):
        m, l, acc = state
        s = lax.dot_general(q, kb, (((1,), (1,)), ((), ())), preferred_element_type=F32) - ck
        if diagonal:
            n = q.shape[0]
            own = s[:, s.shape[1] - n:]
            r_i = lax.broadcasted_iota(jnp.int32, own.shape, 0)
            c_i = lax.broadcasted_iota(jnp.int32, own.shape, 1)
            own = jnp.where(c_i <= r_i, own, NEG_BIG)
            s = own if n == s.shape[1] else jnp.concatenate([s[:, :s.shape[1] - n], own], axis=1)
        m_new = jnp.maximum(m, jnp.max(s, axis=-1, keepdims=True))
        alpha = jnp.exp2(m - m_new)
        p = jnp.exp2(s - m_new)
        l = alpha * l + jnp.sum(p, axis=-1, keepdims=True)
        acc = alpha * acc + jnp.dot(p.astype(BF16), vb, preferred_element_type=F32)
        return m_new, l, acc

    def bias_row(h, key_start, width):
        first = key_start // td
        parts = [c_scr[first + i, pl.ds(hg * hps + h, 1), :] for i in range(width // td)]
        return parts[0] if len(parts) == 1 else jnp.concatenate(parts, axis=1)

    def full_block(j, carry):
        start = j * tk
        return tuple(
            update(carry[h], qs[h], k_ref[pl.ds(start, tk), hs], v_ref[pl.ds(start, tk), hs], bias_row(h, start, tk),
                   False)
            for h, hs in enumerate(heads))

    def attend(tile):
        carry = tuple((jnp.full((tq, 1), NEG_BIG, F32), jnp.zeros((tq, 1), F32), jnp.zeros((tq, LANES), F32))
                      for _ in heads)
        for j in range(tile * tq // tk):
            carry = full_block(j, carry)
        start = tile * tq
        for h, hs in enumerate(heads):
            m, l, acc = carry[h]
            for r in range(tq // td):
                rows = slice(r * td, (r + 1) * td)
                n_keys = (r + 1) * td
                _, l_r, acc_r = update((m[rows], l[rows], acc[rows]), qs[h][rows], k_ref[pl.ds(start, n_keys), hs],
                                       v_ref[pl.ds(start, n_keys), hs], bias_row(h, start, n_keys), True)
                o_ref[rows, hs] = (acc_r / l_r).astype(BF16)

    for tile in range(s_len // tq):
        pl.when(qi == tile)(functools.partial(attend, tile))


def _fox(qkv, flog_t, bf_tab, batch, seq, tq, tk, td, hps, cast_weights, layer):
    assert tq % td == 0 and tk % td == 0 and tq % tk == 0
    nq = seq // tq
    width = hps * LANES
    per_piece = GROUP_WIDTH // width
    n_steps = batch * per_piece * nq

    def step(b, g, i):
        return (b * per_piece + g) * nq + i

    cast_in, cast_out, cast_shapes = [], [], []
    for w in cast_weights:
        _, rows, cols = w.shape
        slab = rows // n_steps
        assert slab * n_steps == rows and slab % BF16_TILE_ROWS == 0, (w.shape, n_steps)
        cast_in.append(pl.BlockSpec((None, slab, cols), lambda b, g, i: (layer, step(b, g, i), 0)))
        cast_out.append(pl.BlockSpec((slab, cols), lambda b, g, i: (step(b, g, i), 0)))
        cast_shapes.append(jax.ShapeDtypeStruct((rows, cols), BF16))

    outs = pl.pallas_call(
        functools.partial(_fox_kernel, tq=tq, tk=tk, td=td, hps=hps, scale=LANES ** -0.5, n_cast=len(cast_weights)),
        grid=(batch, per_piece, nq),
        in_specs=[
            pl.BlockSpec((SUBLANES, seq), lambda b, g, i: (0, b)),
            _resident((SUBLANES, LANES)),
            pl.BlockSpec((tq, width), lambda b, g, i: (b * nq + i, g)),
            pl.BlockSpec((seq, width), lambda b, g, i: (b, per_piece + g)),
            pl.BlockSpec((seq, width), lambda b, g, i: (b, 2 * per_piece + g)),
        ] + cast_in,
        out_specs=[pl.BlockSpec((tq, width), lambda b, g, i: (b * nq + i, g))] + cast_out,
        out_shape=[jax.ShapeDtypeStruct((batch * seq, GROUP_WIDTH), BF16)] + cast_shapes,
        scratch_shapes=[pltpu.VMEM((seq // td, SUBLANES, td), F32)],
        compiler_params=_cparams("parallel", "arbitrary", "arbitrary"),
        name="fox",
    )(flog_t, bf_tab, qkv, qkv, qkv, *cast_weights)
    return outs[0], outs[1:]


def _ret_tables(seq):
    half = RET_QK // 2
    inv = np.exp(-(np.arange(half, dtype=np.float32) / half) * np.float32(np.log(ROPE_BASE))).astype(np.float32)
    ang = np.arange(seq, dtype=np.float32)[:, None] * inv[None, :]
    cos = np.tile(np.cos(ang).astype(np.float32), (1, N_HEADS))
    sin = np.tile(np.sin(ang).astype(np.float32), (1, N_HEADS))
    gamma = (1.0 - np.exp((-5.0 - np.arange(N_HEADS, dtype=np.float32)) * np.float32(np.log(2.0)))).astype(np.float32)
    log_gamma = np.log(gamma).astype(np.float32)
    l = np.arange(CHUNK, dtype=np.float32)
    diff = l[:, None] - l[None, :]
    decay = np.where(diff[None] >= 0, np.exp(np.maximum(diff, 0.0)[None] * log_gamma[:, None, None]), 0.0)
    decay = np.transpose(decay, (1, 0, 2)).reshape(CHUNK, N_HEADS * CHUNK).astype(np.float32)
    xi = np.repeat(np.exp((l + 1.0)[:, None] * log_gamma[None, :]), LANES, axis=1).astype(np.float32)
    lane_head = np.tile(np.repeat(np.arange(N_HEADS), half), 2)
    zeta = np.exp((CHUNK - 1.0 - l)[:, None] * log_gamma[lane_head][None, :]).astype(np.float32)
    hm = np.zeros((SUBLANES, 2 * LANES), np.float32)
    for h in range(N_HEADS):
        hm[h] = lane_head == h
    col_head = np.repeat(np.arange(N_HEADS), LANES)
    cd = np.exp(CHUNK * log_gamma)[col_head][None, :].astype(np.float32)
    bd = (lane_head[:, None] == col_head[None, :]).astype(np.float32)
    return tuple(jnp.asarray(t) for t in (cos, sin, hm, zeta, decay, xi, cd, bd))


def _outproj_kernel(h_ref, ya_ref, yb_ref, yc_ref, yd_ref, w_ref, o_ref, mix_scr):
    for gi, y_ref in enumerate((ya_ref, yb_ref, yc_ref, yd_ref)):
        mix_scr[:, gi * GROUP_WIDTH:(gi + 1) * GROUP_WIDTH] = y_ref[...]
    mix = mix_scr[...]
    for c in range(o_ref.shape[1] // GROUP_WIDTH):
        cols = slice(c * GROUP_WIDTH, (c + 1) * GROUP_WIDTH)
        o_ref[:, cols] = h_ref[:, cols] + jnp.dot(mix, w_ref[:, cols], preferred_element_type=F32)


def _outproj(h, ys, w_out, tm):
    n, d = h.shape
    y_spec = pl.BlockSpec((tm, GROUP_WIDTH), lambda i: (i, 0))
    w_spec = _resident(w_out.shape)
    return pl.pallas_call(
        _outproj_kernel,
        grid=(n // tm,),
        in_specs=[pl.BlockSpec((tm, d), lambda i: (i, 0)), y_spec, y_spec, y_spec, y_spec, w_spec],
        out_specs=pl.BlockSpec((tm, d), lambda i: (i, 0)),
        out_shape=jax.ShapeDtypeStruct((n, d), F32),
        scratch_shapes=[pltpu.VMEM((tm, N_HEADS * GROUP_WIDTH), BF16)],
        compiler_params=_cparams("parallel"),
        name="outproj",
    )(h, *ys, w_out)


def _mlp_kernel(h_ref, g_ref, w1_ref, w2_ref, gf_ref, o_ref, hn_scr, *, final_norm, n_split):
    f = pl.program_id(1)
    sub = h_ref.shape[0] // n_split
    row_groups = [slice(r * sub, (r + 1) * sub) for r in range(n_split)]

    @pl.when(f == 0)
    def _():
        for rows in row_groups:
            x = h_ref[rows, :]
            hn_scr[rows, :] = (x * _rms_scale(x) * g_ref[...]).astype(BF16)
            o_ref[rows, :] = x

    for rows in row_groups:
        a = jnp.dot(hn_scr[rows, :], w1_ref[...], preferred_element_type=F32)
        a = jnp.square(jnp.maximum(a, 0.0)).astype(BF16)
        o_ref[rows, :] += jnp.dot(a, w2_ref[...], preferred_element_type=F32)

    if final_norm:
        @pl.when(f == pl.num_programs(1) - 1)
        def _():
            for rows in row_groups:
                y = o_ref[rows, :]
                o_ref[rows, :] = y * _rms_scale(y) * gf_ref[...]


def _mlp(h, g, w1, w2, g_final, tm, tf, final_norm):
    n, d = h.shape
    d_ff = w1.shape[1]
    return pl.pallas_call(
        functools.partial(_mlp_kernel, final_norm=final_norm, n_split=max(1, tm // ROW_TILE)),
        grid=(n // tm, d_ff // tf),
        in_specs=[
            pl.BlockSpec((tm, d), lambda i, f: (i, 0)),
            _resident((1, d)),
            pl.BlockSpec((d, tf), lambda i, f: (0, f)),
            pl.BlockSpec((tf, d), lambda i, f: (f, 0)),
            _resident((1, d)),
        ],
        out_specs=pl.BlockSpec((tm, d), lambda i, f: (i, 0)),
        out_shape=jax.ShapeDtypeStruct((n, d), F32),
        scratch_shapes=[pltpu.VMEM((tm, d), BF16)],
        compiler_params=_cparams("parallel", "arbitrary"),
        name="mlp",
    )(h, g, w1, w2, g_final)


def kernel(x, norm_mix_g, w_in, fox_b_f, pool_w, pool_scale, sgu_norm_g, sgu_w_s, sgu_b, ret_norm_g, w_out,
           norm_mlp_g, w_ff1, w_ff2, norm_final_g):
    batch, seq, d = x.shape
    depth = w_in.shape[0]
    n = batch * seq
    tm_in = min(ROW_TILE, seq)
    tm_out = tm_mlp = min(WIDE_ROW_TILE, n)
    tf_mlp = FF_TILE
    fox_tiles = (min(FOX_TQ, seq), min(FOX_TK, seq), min(FOX_TD, seq), FOX_HEADS_PER_STEP)
    tables = _ret_tables(seq)

    w_main, w_forget = _wprep(w_in)

    h = x.reshape(n, d)
    g_final = norm_final_g.reshape(1, d)
    for layer in range(depth):
        sgu_bias = jnp.repeat(jnp.transpose(sgu_b[layer]), LANES, axis=1)
        qkv, flog_t, y_a, y_b, y_d = _inmix(
            h, norm_mix_g[layer].reshape(1, d), w_main, w_forget, layer,
            pool_w[layer], pool_scale[layer].reshape(1, GROUP_WIDTH),
            sgu_norm_g[layer].reshape(1, GROUP_WIDTH), sgu_w_s[layer], sgu_bias,
            tables, ret_norm_g[layer].reshape(1, GROUP_WIDTH), seq, tm_in)
        bf_tab = jnp.zeros((SUBLANES, LANES), F32).at[:N_HEADS].set(jnp.broadcast_to(fox_b_f[layer][:, None], (N_HEADS, LANES)))
        y_c, (w_out_b, w_ff1_b, w_ff2_b) = _fox(qkv, flog_t, bf_tab, batch, seq, *fox_tiles,
                                                cast_weights=(w_out, w_ff1, w_ff2), layer=layer)

        h = _outproj(h, (y_a, y_b, y_c, y_d), w_out_b, tm_out)
        h = _mlp(h, norm_mlp_g[layer].reshape(1, d), w_ff1_b, w_ff2_b, g_final, tm_mlp, tf_mlp,
                 final_norm=(layer == depth - 1))
    return h.reshape(batch, seq, d)
```

```python
import functools

import numpy as np
import jax
import jax.numpy as jnp
from jax import lax
from jax.experimental import pallas as pl
from jax.experimental.pallas import tpu as pltpu

F32 = jnp.float32
BF16 = jnp.bfloat16

EPS = 1e-6
NEG_BIG = -1e30
GROUP_WIDTH = 512
LANES = 128
SUBLANES = 8
BF16_TILE_ROWS = 16
ROW_TILE = 512
WIDE_ROW_TILE = 1024
FF_TILE = 1024
N_HEADS = 4
POOL_WINDOWS = (2, 4, 8, 16)
CHUNK = 128
POOL_HALO = 16
RET_QK = 64
ROPE_BASE = 10000.0
N_MAIN = 9 * GROUP_WIDTH
F_COL0 = 6 * GROUP_WIDTH
W_PREP_COLS = 256
V7X_VMEM_LIMIT = 56 * 1024 * 1024
FOX_TQ, FOX_TK, FOX_TD, FOX_HEADS_PER_STEP = ROW_TILE, ROW_TILE, ROW_TILE, N_HEADS


def _cparams(*sem):
    return pltpu.CompilerParams(dimension_semantics=sem, vmem_limit_bytes=V7X_VMEM_LIMIT)


def _resident(shape):
    nd = len(shape)
    return pl.BlockSpec(shape, lambda *_: (0,) * nd, pipeline_mode=pl.Buffered(1))


def _rms_scale(x):
    return lax.rsqrt(jnp.mean(x * x, axis=-1, keepdims=True) + EPS)


def _wprep_kernel(w_ref, f_ref, w_out_ref, f_out_ref):
    t = pl.program_id(0)
    n_layers = w_ref.shape[1]
    half = RET_QK // 2

    def emit(row_groups):
        for layer in range(n_layers):
            for b, ranges in enumerate(row_groups):
                x = jnp.concatenate([w_ref[r0:r1, layer, :] for r0, r1 in ranges], axis=0)
                w_out_ref[layer, :, b * LANES:(b + 1) * LANES] = jnp.transpose(x).astype(BF16)

    n_blocks = W_PREP_COLS // LANES
    retention_qk = (t >= F_COL0 // W_PREP_COLS) & (t < (F_COL0 + 2 * N_HEADS * RET_QK) // W_PREP_COLS)

    @pl.when(jnp.logical_not(retention_qk))
    def _():
        emit([[(b * LANES, (b + 1) * LANES)] for b in range(n_blocks)])

    @pl.when(retention_qk)
    def _():
        emit([[(hd * RET_QK + hf * half, hd * RET_QK + (hf + 1) * half) for hd in range(N_HEADS)]
              for hf in range(n_blocks)])

    keep = lax.broadcasted_iota(jnp.int32, f_out_ref.shape[1:], 0) < N_HEADS
    for layer in range(n_layers):
        f_out_ref[layer] = jnp.where(keep, f_ref[:, layer, :], 0.0)


def _wprep(w_in):
    n_layers, d, _ = w_in.shape
    assert W_PREP_COLS == N_HEADS * RET_QK and F_COL0 % W_PREP_COLS == 0
    w_t = jnp.transpose(w_in, (2, 0, 1))

    def src_row(t):
        return t * W_PREP_COLS + jnp.where(t >= F_COL0 // W_PREP_COLS, N_HEADS, 0)

    return pl.pallas_call(
        _wprep_kernel,
        grid=(N_MAIN // W_PREP_COLS,),
        in_specs=[
            pl.BlockSpec((pl.Element(W_PREP_COLS), pl.Element(n_layers), pl.Element(d)), lambda t: (src_row(t), 0, 0)),
            pl.BlockSpec((pl.Element(SUBLANES), pl.Element(n_layers), pl.Element(d)), lambda t: (F_COL0, 0, 0)),
        ],
        out_specs=[
            pl.BlockSpec((n_layers, d, W_PREP_COLS), lambda t: (0, 0, t)),
            pl.BlockSpec((n_layers, SUBLANES, d), lambda t: (0, 0, 0)),
        ],
        out_shape=[jax.ShapeDtypeStruct((n_layers, d, N_MAIN), BF16),
                   jax.ShapeDtypeStruct((n_layers, SUBLANES, d), F32)],
        compiler_params=_cparams("arbitrary"),
        name="wprep",
    )(w_t, w_t)


def _head_norm(x):
    mu = jnp.mean(x, axis=-1, keepdims=True)
    d = x - mu
    var = jnp.mean(d * d, axis=-1, keepdims=True)
    return d * lax.rsqrt(var + EPS)


def _pool_windows(a, halo, pos0):
    tm = a.shape[0]
    ext = jnp.concatenate([halo, a], axis=0)
    pos = lax.broadcasted_iota(jnp.int32, (tm, LANES), 0) + pos0
    ps = []
    for gi, w in enumerate(POOL_WINDOWS):
        cols = slice(gi * LANES, (gi + 1) * LANES)
        tot = ext[:, cols]
        sh = 1
        while sh < w:
            tot = tot + pltpu.roll(tot, sh, axis=0)
            sh *= 2
        cnt = jnp.minimum(pos + 1, w).astype(F32)
        ps.append((tot[POOL_HALO:] / cnt - a[:, cols]).astype(BF16))
    return ps


def _pool_mix(ps, pw_ref, ps_ref):
    ys = [jnp.dot(p, pw_ref[gi].astype(BF16), preferred_element_type=F32) for gi, p in enumerate(ps)]
    return jnp.concatenate(ys, axis=1) * ps_ref[...]


def _sgu_gates(u, v, ng_ref):
    v = jax.nn.gelu(v)
    vns = [(_head_norm(v[:, h * LANES:(h + 1) * LANES]) * ng_ref[:, h * LANES:(h + 1) * LANES]).astype(BF16)
           for h in range(N_HEADS)]
    return jax.nn.gelu(u), vns


def _sgu_mix(gates, w_causal, bias_ref):
    zs = [jnp.dot(w_causal[h], jnp.concatenate([vns[h] for _, vns in gates], axis=1), preferred_element_type=F32)
          for h in range(N_HEADS)]
    outs = []
    for ci, (gu, _) in enumerate(gates):
        z = jnp.concatenate([zs[h][:, ci * LANES:(ci + 1) * LANES] for h in range(N_HEADS)], axis=1)
        outs.append(gu * (z + bias_ref[...]))
    return outs


def _rope(x, cos, sin):
    x1, x2 = x[:, :LANES], x[:, LANES:]
    return jnp.concatenate([x1 * cos - x2 * sin, x1 * sin + x2 * cos], axis=1)


def _retention_scores(q, k, cos, sin, hm_ref, decay_ref, zeta_ref):
    qb = _rope(q, cos, sin).astype(BF16)
    kr = _rope(k, cos, sin) * (RET_QK ** -0.5)
    k_heads = jnp.concatenate([(kr * hm_ref[h:h + 1, :]).astype(BF16) for h in range(N_HEADS)], axis=0)
    s = lax.dot_general(qb, k_heads, (((1,), (1,)), ((), ())), preferred_element_type=F32)
    kz_t = jnp.transpose(kr * zeta_ref[...]).astype(BF16)
    return qb, (s * decay_ref[...]).astype(BF16), kz_t


def _retention_mix(qb, sb, kz_t, v, g, state_scr, xi_ref, cd_ref, bd_ref, ng_ref):
    y_intra = jnp.concatenate(
        [jnp.dot(sb[:, h * LANES:(h + 1) * LANES], v[:, h * LANES:(h + 1) * LANES], preferred_element_type=F32)
         for h in range(N_HEADS)], axis=1)
    state = state_scr[...]
    y_cross = jnp.dot(qb, state.astype(BF16), preferred_element_type=F32) * xi_ref[...]
    upd = jnp.dot(kz_t, v, preferred_element_type=F32)
    state_scr[...] = state * cd_ref[...] + upd * bd_ref[...]
    y = y_intra + y_cross
    yn = jnp.concatenate([_head_norm(y[:, h * LANES:(h + 1) * LANES]) for h in range(N_HEADS)], axis=1)
    return jax.nn.silu(g) * (yn * ng_ref[...])


def _inmix_kernel(h_ref, g_ref, w_ref, wf_ref, pool_w_ref, pool_s_ref, sgu_g_ref, sgu_w_ref, sgu_b_ref,
                  cos_ref, sin_ref, hm_ref, zeta_ref, decay_ref, xi_ref, cd_ref, bd_ref, ret_g_ref,
                  qkv_ref, flog_ref, ya_ref, yb_ref, yd_ref, state_scr, halo_scr, *, tiles_per_seq):
    tm = h_ref.shape[0]
    tile_in_seq = pl.program_id(0) % tiles_per_seq

    @pl.when(tile_in_seq == 0)
    def _():
        state_scr[...] = jnp.zeros_like(state_scr)
        halo_scr[...] = jnp.zeros_like(halo_scr)

    x = h_ref[...]
    xf = x * _rms_scale(x) * g_ref[...]
    xn = xf.astype(BF16)

    def project(piece):
        cols = slice(piece * GROUP_WIDTH, (piece + 1) * GROUP_WIDTH)
        return jnp.dot(xn, w_ref[:, cols], preferred_element_type=F32)

    def emit_qkv(piece):
        qkv_ref[:, piece * GROUP_WIDTH:(piece + 1) * GROUP_WIDTH] = project(3 + piece).astype(BF16)

    chunks = [slice(ci * CHUNK, (ci + 1) * CHUNK) for ci in range(tm // CHUNK)]
    assert len(chunks) == 4

    a = project(0)
    pool_ops = _pool_windows(a, halo_scr[...], tile_in_seq * tm)
    halo_scr[...] = a[tm - POOL_HALO:, :]
    qk = project(6)
    ret_ops = [_retention_scores(qk[rows, :2 * LANES], qk[rows, 2 * LANES:], cos_ref[rows, :], sin_ref[rows, :],
                                 hm_ref, decay_ref, zeta_ref) for rows in chunks]
    rv, rg = project(7).astype(BF16), project(8)
    ya_ref[...] = _pool_mix(pool_ops, pool_w_ref, pool_s_ref).astype(BF16)

    def retention(ci):
        rows = chunks[ci]
        yd_ref[rows, :] = _retention_mix(*ret_ops[ci], rv[rows], rg[rows], state_scr, xi_ref, cd_ref, bd_ref,
                                         ret_g_ref).astype(BF16)

    u = project(1)
    retention(0)
    v = project(2)
    retention(1)
    sgu_ops = [_sgu_gates(u[rows], v[rows], sgu_g_ref) for rows in chunks]
    emit_qkv(0)
    retention(2)
    emit_qkv(1)
    retention(3)
    emit_qkv(2)
    r_i = lax.broadcasted_iota(jnp.int32, (CHUNK, CHUNK), 0)
    c_i = lax.broadcasted_iota(jnp.int32, (CHUNK, CHUNK), 1)
    w_causal = [jnp.where(c_i <= r_i, sgu_w_ref[h], 0.0).astype(BF16) for h in range(N_HEADS)]
    for rows, y in zip(chunks, _sgu_mix(sgu_ops, w_causal, sgu_b_ref)):
        yb_ref[rows, :] = y.astype(BF16)

    lane = lax.broadcasted_iota(jnp.int32, (1, LANES), 1)
    logits = jnp.zeros((tm, LANES), F32)
    for h in range(N_HEADS):
        r = jnp.sum(xf * wf_ref[h:h + 1, :], axis=-1, keepdims=True)
        logits = logits + r * (lane == h).astype(F32)
    flog_ref[...] = jnp.transpose(logits)[:flog_ref.shape[0], :]


def _inmix(h, g, w_main, wf, layer, pool_w, pool_scale, sgu_g, sgu_w, sgu_bias, ret_tables, ret_g, seq, tm):
    n, d = h.shape
    assert seq % tm == 0 and tm % CHUNK == 0
    tiles_per_seq = seq // tm
    cos, sin, hm, zeta, decay, xi, cd, bd = ret_tables

    def layer_resident(shape):
        return pl.BlockSpec((None,) + shape, lambda i: (layer, 0, 0), pipeline_mode=pl.Buffered(1))

    def rows(width):
        return pl.BlockSpec((tm, width), lambda i: (i, 0))

    pos_rows = pl.BlockSpec((tm, LANES), lambda i: (i % tiles_per_seq, 0))
    mix_out = jax.ShapeDtypeStruct((n, GROUP_WIDTH), BF16)
    return pl.pallas_call(
        functools.partial(_inmix_kernel, tiles_per_seq=tiles_per_seq),
        grid=(n // tm,),
        in_specs=[rows(d), _resident((1, d)), layer_resident(w_main.shape[1:]), layer_resident(wf.shape[1:]),
                  _resident(pool_w.shape), _resident((1, GROUP_WIDTH)),
                  _resident((1, GROUP_WIDTH)), _resident(sgu_w.shape), _resident((CHUNK, GROUP_WIDTH)),
                  pos_rows, pos_rows, _resident(hm.shape), _resident(zeta.shape), _resident(decay.shape),
                  _resident(xi.shape), _resident(cd.shape), _resident(bd.shape), _resident((1, GROUP_WIDTH))],
        out_specs=[rows(3 * GROUP_WIDTH), pl.BlockSpec((SUBLANES, tm), lambda i: (0, i)),
                   rows(GROUP_WIDTH), rows(GROUP_WIDTH), rows(GROUP_WIDTH)],
        out_shape=[jax.ShapeDtypeStruct((n, 3 * GROUP_WIDTH), BF16), jax.ShapeDtypeStruct((SUBLANES, n), F32),
                   mix_out, mix_out, mix_out],
        scratch_shapes=[pltpu.VMEM((2 * LANES, GROUP_WIDTH), F32), pltpu.VMEM((POOL_HALO, GROUP_WIDTH), F32)],
        compiler_params=_cparams("arbitrary"),
        name="inmix",
    )(h, g, w_main, wf, pool_w, pool_scale, sgu_g, sgu_w, sgu_bias, cos, sin, hm, zeta, decay, xi, cd, bd, ret_g)


def _fox_kernel(flog_ref, bf_ref, q_ref, k_ref, v_ref, *rest, tq, tk, td, hps, scale, n_cast):
    cast_src, o_ref, cast_dst, c_scr = rest[:n_cast], rest[n_cast], rest[n_cast + 1:2 * n_cast + 1], rest[-1]
    hg = pl.program_id(1)
    qi = pl.program_id(2)
    s_len = k_ref.shape[0]
    log2e = float(np.log2(np.e))

    for src, dst in zip(cast_src, cast_dst):
        dst[...] = src[...].astype(BF16)

    @pl.when((hg == 0) & (qi == 0))
    def _():
        x = flog_ref[...] + bf_ref[:, 0:1]
        c = jnp.minimum(x, 0.0) - jnp.log1p(jnp.exp(-jnp.abs(x)))
        lane = lax.broadcasted_iota(jnp.int32, c.shape, 1)
        sh = 1
        while sh < s_len:
            c = c + jnp.where(lane >= sh, pltpu.roll(c, sh, axis=1), 0.0)
            sh *= 2
        c = c * log2e
        for j in range(s_len // td):
            c_scr[j] = c[:, j * td:(j + 1) * td]

    heads = [slice(h * LANES, (h + 1) * LANES) for h in range(hps)]
    qs = [(q_ref[:, hs].astype(F32) * (scale * log2e)).astype(BF16) for hs in heads]

    def update(state, q, kb, vb, ck, diagonal):
        m, l, acc = state
        s = lax.dot_general(q, kb, (((1,), (1,)), ((), ())), preferred_element_type=F32) - ck
        if diagonal:
            n = q.shape[0]
            own = s[:, s.shape[1] - n:]
            r_i = lax.broadcasted_iota(jnp.int32, own.shape, 0)
            c_i = lax.broadcasted_iota(jnp.int32, own.shape, 1)
            own = jnp.where(c_i <= r_i, own, NEG_BIG)
            s = own if n == s.shape[1] else jnp.concatenate([s[:, :s.shape[1] - n], own], axis=1)
        m_new = jnp.maximum(m, jnp.max(s, axis=-1, keepdims=True))
        alpha = jnp.exp2(m - m_new)
        p = jnp.exp2(s - m_new)
        l = alpha * l + jnp.sum(p, axis=-1, keepdims=True)
        acc = alpha * acc + jnp.dot(p.astype(BF16), vb, preferred_element_type=F32)
        return m_new, l, acc

    def bias_row(h, key_start, width):
        first = key_start // td
        parts = [c_scr[first + i, pl.ds(hg * hps + h, 1), :] for i in range(width // td)]
        return parts[0] if len(parts) == 1 else jnp.concatenate(parts, axis=1)

    def full_block(j, carry):
        start = j * tk
        return tuple(
            update(carry[h], qs[h], k_ref[pl.ds(start, tk), hs], v_ref[pl.ds(start, tk), hs], bias_row(h, start, tk),
                   False)
            for h, hs in enumerate(heads))

    def attend(tile):
        carry = tuple((jnp.full((tq, 1), NEG_BIG, F32), jnp.zeros((tq, 1), F32), jnp.zeros((tq, LANES), F32))
                      for _ in heads)
        for j in range(tile * tq // tk):
            carry = full_block(j, carry)
        start = tile * tq
        for h, hs in enumerate(heads):
            m, l, acc = carry[h]
            for r in range(tq // td):
                rows = slice(r * td, (r + 1) * td)
                n_keys = (r + 1) * td
                _, l_r, acc_r = update((m[rows], l[rows], acc[rows]), qs[h][rows], k_ref[pl.ds(start, n_keys), hs],
                                       v_ref[pl.ds(start, n_keys), hs], bias_row(h, start, n_keys), True)
                o_ref[rows, hs] = (acc_r / l_r).astype(BF16)

    for tile in range(s_len // tq):
        pl.when(qi == tile)(functools.partial(attend, tile))


def _fox(qkv, flog_t, bf_tab, batch, seq, tq, tk, td, hps, cast_weights, layer):
    assert tq % td == 0 and tk % td == 0 and tq % tk == 0
    nq = seq // tq
    width = hps * LANES
    per_piece = GROUP_WIDTH // width
    n_steps = batch * per_piece * nq

    def step(b, g, i):
        return (b * per_piece + g) * nq + i

    cast_in, cast_out, cast_shapes = [], [], []
    for w in cast_weights:
        _, rows, cols = w.shape
        slab = rows // n_steps
        assert slab * n_steps == rows and slab % BF16_TILE_ROWS == 0, (w.shape, n_steps)
        cast_in.append(pl.BlockSpec((None, slab, cols), lambda b, g, i: (layer, step(b, g, i), 0)))
        cast_out.append(pl.BlockSpec((slab, cols), lambda b, g, i: (step(b, g, i), 0)))
        cast_shapes.append(jax.ShapeDtypeStruct((rows, cols), BF16))

    outs = pl.pallas_call(
        functools.partial(_fox_kernel, tq=tq, tk=tk, td=td, hps=hps, scale=LANES ** -0.5, n_cast=len(cast_weights)),
        grid=(batch, per_piece, nq),
        in_specs=[
            pl.BlockSpec((SUBLANES, seq), lambda b, g, i: (0, b)),
            _resident((SUBLANES, LANES)),
            pl.BlockSpec((tq, width), lambda b, g, i: (b * nq + i, g)),
            pl.BlockSpec((seq, width), lambda b, g, i: (b, per_piece + g)),
            pl.BlockSpec((seq, width), lambda b, g, i: (b, 2 * per_piece + g)),
        ] + cast_in,
        out_specs=[pl.BlockSpec((tq, width), lambda b, g, i: (b * nq + i, g))] + cast_out,
        out_shape=[jax.ShapeDtypeStruct((batch * seq, GROUP_WIDTH), BF16)] + cast_shapes,
        scratch_shapes=[pltpu.VMEM((seq // td, SUBLANES, td), F32)],
        compiler_params=_cparams("parallel", "arbitrary", "arbitrary"),
        name="fox",
    )(flog_t, bf_tab, qkv, qkv, qkv, *cast_weights)
    return outs[0], outs[1:]


def _ret_tables(seq):
    half = RET_QK // 2
    inv = np.exp(-(np.arange(half, dtype=np.float32) / half) * np.float32(np.log(ROPE_BASE))).astype(np.float32)
    ang = np.arange(seq, dtype=np.float32)[:, None] * inv[None, :]
    cos = np.tile(np.cos(ang).astype(np.float32), (1, N_HEADS))
    sin = np.tile(np.sin(ang).astype(np.float32), (1, N_HEADS))
    gamma = (1.0 - np.exp((-5.0 - np.arange(N_HEADS, dtype=np.float32)) * np.float32(np.log(2.0)))).astype(np.float32)
    log_gamma = np.log(gamma).astype(np.float32)
    l = np.arange(CHUNK, dtype=np.float32)
    diff = l[:, None] - l[None, :]
    decay = np.where(diff[None] >= 0, np.exp(np.maximum(diff, 0.0)[None] * log_gamma[:, None, None]), 0.0)
    decay = np.transpose(decay, (1, 0, 2)).reshape(CHUNK, N_HEADS * CHUNK).astype(np.float32)
    xi = np.repeat(np.exp((l + 1.0)[:, None] * log_gamma[None, :]), LANES, axis=1).astype(np.float32)
    lane_head = np.tile(np.repeat(np.arange(N_HEADS), half), 2)
    zeta = np.exp((CHUNK - 1.0 - l)[:, None] * log_gamma[lane_head][None, :]).astype(np.float32)
    hm = np.zeros((SUBLANES, 2 * LANES), np.float32)
    for h in range(N_HEADS):
        hm[h] = lane_head == h
    col_head = np.repeat(np.arange(N_HEADS), LANES)
    cd = np.exp(CHUNK * log_gamma)[col_head][None, :].astype(np.float32)
    bd = (lane_head[:, None] == col_head[None, :]).astype(np.float32)
    return tuple(jnp.asarray(t) for t in (cos, sin, hm, zeta, decay, xi, cd, bd))


def _outproj_kernel(h_ref, ya_ref, yb_ref, yc_ref, yd_ref, w_ref, o_ref, mix_scr):
    for gi, y_ref in enumerate((ya_ref, yb_ref, yc_ref, yd_ref)):
        mix_scr[:, gi * GROUP_WIDTH:(gi + 1) * GROUP_WIDTH] = y_ref[...]
    mix = mix_scr[...]
    for c in range(o_ref.shape[1] // GROUP_WIDTH):
        cols = slice(c * GROUP_WIDTH, (c + 1) * GROUP_WIDTH)
        o_ref[:, cols] = h_ref[:, cols] + jnp.dot(mix, w_ref[:, cols], preferred_element_type=F32)


def _outproj(h, ys, w_out, tm):
    n, d = h.shape
    y_spec = pl.BlockSpec((tm, GROUP_WIDTH), lambda i: (i, 0))
    w_spec = _resident(w_out.shape)
    return pl.pallas_call(
        _outproj_kernel,
        grid=(n // tm,),
        in_specs=[pl.BlockSpec((tm, d), lambda i: (i, 0)), y_spec, y_spec, y_spec, y_spec, w_spec],
        out_specs=pl.BlockSpec((tm, d), lambda i: (i, 0)),
        out_shape=jax.ShapeDtypeStruct((n, d), F32),
        scratch_shapes=[pltpu.VMEM((tm, N_HEADS * GROUP_WIDTH), BF16)],
        compiler_params=_cparams("parallel"),
        name="outproj",
    )(h, *ys, w_out)


def _mlp_kernel(h_ref, g_ref, w1_ref, w2_ref, gf_ref, o_ref, hn_scr, *, final_norm, n_split):
    f = pl.program_id(1)
    sub = h_ref.shape[0] // n_split
    row_groups = [slice(r * sub, (r + 1) * sub) for r in range(n_split)]

    @pl.when(f == 0)
    def _():
        for rows in row_groups:
            x = h_ref[rows, :]
            hn_scr[rows, :] = (x * _rms_scale(x) * g_ref[...]).astype(BF16)
            o_ref[rows, :] = x

    for rows in row_groups:
        a = jnp.dot(hn_scr[rows, :], w1_ref[...], preferred_element_type=F32)
        a = jnp.square(jnp.maximum(a, 0.0)).astype(BF16)
        o_ref[rows, :] += jnp.dot(a, w2_ref[...], preferred_element_type=F32)

    if final_norm:
        @pl.when(f == pl.num_programs(1) - 1)
        def _():
            for rows in row_groups:
                y = o_ref[rows, :]
                o_ref[rows, :] = y * _rms_scale(y) * gf_ref[...]


def _mlp(h, g, w1, w2, g_final, tm, tf, final_norm):
    n, d = h.shape
    d_ff = w1.shape[1]
    return pl.pallas_call(
        functools.partial(_mlp_kernel, final_norm=final_norm, n_split=max(1, tm // ROW_TILE)),
        grid=(n // tm, d_ff // tf),
        in_specs=[
            pl.BlockSpec((tm, d), lambda i, f: (i, 0)),
            _resident((1, d)),
            pl.BlockSpec((d, tf), lambda i, f: (0, f)),
            pl.BlockSpec((tf, d), lambda i, f: (f, 0)),
            _resident((1, d)),
        ],
        out_specs=pl.BlockSpec((tm, d), lambda i, f: (i, 0)),
        out_shape=jax.ShapeDtypeStruct((n, d), F32),
        scratch_shapes=[pltpu.VMEM((tm, d), BF16)],
        compiler_params=_cparams("parallel", "arbitrary"),
        name="mlp",
    )(h, g, w1, w2, g_final)


def kernel(x, norm_mix_g, w_in, fox_b_f, pool_w, pool_scale, sgu_norm_g, sgu_w_s, sgu_b, ret_norm_g, w_out,
           norm_mlp_g, w_ff1, w_ff2, norm_final_g):
    batch, seq, d = x.shape
    depth = w_in.shape[0]
    n = batch * seq
    tm_in = min(ROW_TILE, seq)
    tm_out = tm_mlp = min(WIDE_ROW_TILE, n)
    tf_mlp = FF_TILE
    fox_tiles = (min(FOX_TQ, seq), min(FOX_TK, seq), min(FOX_TD, seq), FOX_HEADS_PER_STEP)
    tables = _ret_tables(seq)

    w_main, w_forget = _wprep(w_in)

    h = x.reshape(n, d)
    g_final = norm_final_g.reshape(1, d)
    for layer in range(depth):
        sgu_bias = jnp.repeat(jnp.transpose(sgu_b[layer]), LANES, axis=1)
        qkv, flog_t, y_a, y_b, y_d = _inmix(
            h, norm_mix_g[layer].reshape(1, d), w_main, w_forget, layer,
            pool_w[layer], pool_scale[layer].reshape(1, GROUP_WIDTH),
            sgu_norm_g[layer].reshape(1, GROUP_WIDTH), sgu_w_s[layer], sgu_bias,
            tables, ret_norm_g[layer].reshape(1, GROUP_WIDTH), seq, tm_in)
        bf_tab = jnp.zeros((SUBLANES, LANES), F32).at[:N_HEADS].set(jnp.broadcast_to(fox_b_f[layer][:, None], (N_HEADS, LANES)))
        y_c, (w_out_b, w_ff1_b, w_ff2_b) = _fox(qkv, flog_t, bf_tab, batch, seq, *fox_tiles,
                                                cast_weights=(w_out, w_ff1, w_ff2), layer=layer)

        h = _outproj(h, (y_a, y_b, y_c, y_d), w_out_b, tm_out)
        h = _mlp(h, norm_mlp_g[layer].reshape(1, d), w_ff1_b, w_ff2_b, g_final, tm_mlp, tf_mlp,
                 final_norm=(layer == depth - 1))
    return h.reshape(batch, seq, d)
```

```python
import functools

import numpy as np
import jax
import jax.numpy as jnp
from jax import lax
from jax.experimental import pallas as pl
from jax.experimental.pallas import tpu as pltpu

F32 = jnp.float32
BF16 = jnp.bfloat16

EPS = 1e-6
NEG_BIG = -1e30
GROUP_WIDTH = 512
LANES = 128
SUBLANES = 8
BF16_TILE_ROWS = 16
ROW_TILE = 512
WIDE_ROW_TILE = 1024
FF_TILE = 1024
N_HEADS = 4
POOL_WINDOWS = (2, 4, 8, 16)
CHUNK = 128
POOL_HALO = 16
RET_QK = 64
ROPE_BASE = 10000.0
N_MAIN = 9 * GROUP_WIDTH
F_COL0 = 6 * GROUP_WIDTH
W_PREP_COLS = 256
V7X_VMEM_LIMIT = 56 * 1024 * 1024
FOX_TQ, FOX_TK, FOX_TD, FOX_HEADS_PER_STEP = ROW_TILE, ROW_TILE, ROW_TILE, N_HEADS


def _cparams(*sem):
    return pltpu.CompilerParams(dimension_semantics=sem, vmem_limit_bytes=V7X_VMEM_LIMIT)


def _resident(shape):
    nd = len(shape)
    return pl.BlockSpec(shape, lambda *_: (0,) * nd, pipeline_mode=pl.Buffered(1))


def _rms_scale(x):
    return lax.rsqrt(jnp.mean(x * x, axis=-1, keepdims=True) + EPS)


def _wprep_kernel(w_ref, f_ref, w_out_ref, f_out_ref):
    t = pl.program_id(0)
    n_layers = w_ref.shape[1]
    half = RET_QK // 2

    def emit(row_groups):
        for layer in range(n_layers):
            for b, ranges in enumerate(row_groups):
                x = jnp.concatenate([w_ref[r0:r1, layer, :] for r0, r1 in ranges], axis=0)
                w_out_ref[layer, :, b * LANES:(b + 1) * LANES] = jnp.transpose(x).astype(BF16)

    n_blocks = W_PREP_COLS // LANES
    retention_qk = (t >= F_COL0 // W_PREP_COLS) & (t < (F_COL0 + 2 * N_HEADS * RET_QK) // W_PREP_COLS)

    @pl.when(jnp.logical_not(retention_qk))
    def _():
        emit([[(b * LANES, (b + 1) * LANES)] for b in range(n_blocks)])

    @pl.when(retention_qk)
    def _():
        emit([[(hd * RET_QK + hf * half, hd * RET_QK + (hf + 1) * half) for hd in range(N_HEADS)]
              for hf in range(n_blocks)])

    keep = lax.broadcasted_iota(jnp.int32, f_out_ref.shape[1:], 0) < N_HEADS
    for layer in range(n_layers):
        f_out_ref[layer] = jnp.where(keep, f_ref[:, layer, :], 0.0)


def _wprep(w_in):
    n_layers, d, _ = w_in.shape
    assert W_PREP_COLS == N_HEADS * RET_QK and F_COL0 % W_PREP_COLS == 0
    w_t = jnp.transpose(w_in, (2, 0, 1))

    def src_row(t):
        return t * W_PREP_COLS + jnp.where(t >= F_COL0 // W_PREP_COLS, N_HEADS, 0)

    return pl.pallas_call(
        _wprep_kernel,
        grid=(N_MAIN // W_PREP_COLS,),
        in_specs=[
            pl.BlockSpec((pl.Element(W_PREP_COLS), pl.Element(n_layers), pl.Element(d)), lambda t: (src_row(t), 0, 0)),
            pl.BlockSpec((pl.Element(SUBLANES), pl.Element(n_layers), pl.Element(d)), lambda t: (F_COL0, 0, 0)),
        ],
        out_specs=[
            pl.BlockSpec((n_layers, d, W_PREP_COLS), lambda t: (0, 0, t)),
            pl.BlockSpec((n_layers, SUBLANES, d), lambda t: (0, 0, 0)),
        ],
        out_shape=[jax.ShapeDtypeStruct((n_layers, d, N_MAIN), BF16),
                   jax.ShapeDtypeStruct((n_layers, SUBLANES, d), F32)],
        compiler_params=_cparams("arbitrary"),
        name="wprep",
    )(w_t, w_t)


def _head_norm(x):
    mu = jnp.mean(x, axis=-1, keepdims=True)
    d = x - mu
    var = jnp.mean(d * d, axis=-1, keepdims=True)
    return d * lax.rsqrt(var + EPS)


def _pool_windows(a, halo, pos0):
    tm = a.shape[0]
    ext = jnp.concatenate([halo, a], axis=0)
    pos = lax.broadcasted_iota(jnp.int32, (tm, LANES), 0) + pos0
    ps = []
    for gi, w in enumerate(POOL_WINDOWS):
        cols = slice(gi * LANES, (gi + 1) * LANES)
        tot = ext[:, cols]
        sh = 1
        while sh < w:
            tot = tot + pltpu.roll(tot, sh, axis=0)
            sh *= 2
        cnt = jnp.minimum(pos + 1, w).astype(F32)
        ps.append((tot[POOL_HALO:] / cnt - a[:, cols]).astype(BF16))
    return ps


def _pool_mix(ps, pw_ref, ps_ref):
    ys = [jnp.dot(p, pw_ref[gi].astype(BF16), preferred_element_type=F32) for gi, p in enumerate(ps)]
    return jnp.concatenate(ys, axis=1) * ps_ref[...]


def _sgu_gates(u, v, ng_ref):
    v = jax.nn.gelu(v)
    vns = [(_head_norm(v[:, h * LANES:(h + 1) * LANES]) * ng_ref[:, h * LANES:(h + 1) * LANES]).astype(BF16)
           for h in range(N_HEADS)]
    return jax.nn.gelu(u), vns


def _sgu_mix(gates, w_causal, bias_ref):
    zs = [jnp.dot(w_causal[h], jnp.concatenate([vns[h] for _, vns in gates], axis=1), preferred_element_type=F32)
          for h in range(N_HEADS)]
    outs = []
    for ci, (gu, _) in enumerate(gates):
        z = jnp.concatenate([zs[h][:, ci * LANES:(ci + 1) * LANES] for h in range(N_HEADS)], axis=1)
        outs.append(gu * (z + bias_ref[...]))
    return outs


def _rope(x, cos, sin):
    x1, x2 = x[:, :LANES], x[:, LANES:]
    return jnp.concatenate([x1 * cos - x2 * sin, x1 * sin + x2 * cos], axis=1)


def _retention_scores(q, k, cos, sin, hm_ref, decay_ref, zeta_ref):
    qb = _rope(q, cos, sin).astype(BF16)
    kr = _rope(k, cos, sin) * (RET_QK ** -0.5)
    k_heads = jnp.concatenate([(kr * hm_ref[h:h + 1, :]).astype(BF16) for h in range(N_HEADS)], axis=0)
    s = lax.dot_general(qb, k_heads, (((1,), (1,)), ((), ())), preferred_element_type=F32)
    kz_t = jnp.transpose(kr * zeta_ref[...]).astype(BF16)
    return qb, (s * decay_ref[...]).astype(BF16), kz_t


def _retention_mix(qb, sb, kz_t, v, g, state_scr, xi_ref, cd_ref, bd_ref, ng_ref):
    y_intra = jnp.concatenate(
        [jnp.dot(sb[:, h * LANES:(h + 1) * LANES], v[:, h * LANES:(h + 1) * LANES], preferred_element_type=F32)
         for h in range(N_HEADS)], axis=1)
    state = state_scr[...]
    y_cross = jnp.dot(qb, state.astype(BF16), preferred_element_type=F32) * xi_ref[...]
    upd = jnp.dot(kz_t, v, preferred_element_type=F32)
    state_scr[...] = state * cd_ref[...] + upd * bd_ref[...]
    y = y_intra + y_cross
    yn = jnp.concatenate([_head_norm(y[:, h * LANES:(h + 1) * LANES]) for h in range(N_HEADS)], axis=1)
    return jax.nn.silu(g) * (yn * ng_ref[...])


def _inmix_kernel(h_ref, g_ref, w_ref, wf_ref, pool_w_ref, pool_s_ref, sgu_g_ref, sgu_w_ref, sgu_b_ref,
                  cos_ref, sin_ref, hm_ref, zeta_ref, decay_ref, xi_ref, cd_ref, bd_ref, ret_g_ref,
                  qkv_ref, flog_ref, ya_ref, yb_ref, yd_ref, state_scr, halo_scr, *, tiles_per_seq):
    tm = h_ref.shape[0]
    tile_in_seq = pl.program_id(0) % tiles_per_seq

    @pl.when(tile_in_seq == 0)
    def _():
        state_scr[...] = jnp.zeros_like(state_scr)
        halo_scr[...] = jnp.zeros_like(halo_scr)

    x = h_ref[...]
    xf = x * _rms_scale(x) * g_ref[...]
    xn = xf.astype(BF16)

    def project(piece):
        cols = slice(piece * GROUP_WIDTH, (piece + 1) * GROUP_WIDTH)
        return jnp.dot(xn, w_ref[:, cols], preferred_element_type=F32)

    def emit_qkv(piece):
        qkv_ref[:, piece * GROUP_WIDTH:(piece + 1) * GROUP_WIDTH] = project(3 + piece).astype(BF16)

    chunks = [slice(ci * CHUNK, (ci + 1) * CHUNK) for ci in range(tm // CHUNK)]
    assert len(chunks) == 4

    a = project(0)
    pool_ops = _pool_windows(a, halo_scr[...], tile_in_seq * tm)
    halo_scr[...] = a[tm - POOL_HALO:, :]
    qk = project(6)
    ret_ops = [_retention_scores(qk[rows, :2 * LANES], qk[rows, 2 * LANES:], cos_ref[rows, :], sin_ref[rows, :],
                                 hm_ref, decay_ref, zeta_ref) for rows in chunks]
    rv, rg = project(7).astype(BF16), project(8)
    ya_ref[...] = _pool_mix(pool_ops, pool_w_ref, pool_s_ref).astype(BF16)

    def retention(ci):
        rows = chunks[ci]
        yd_ref[rows, :] = _retention_mix(*ret_ops[ci], rv[rows], rg[rows], state_scr, xi_ref, cd_ref, bd_ref,
                                         ret_g_ref).astype(BF16)

    u = project(1)
    retention(0)
    v = project(2)
    retention(1)
    sgu_ops = [_sgu_gates(u[rows], v[rows], sgu_g_ref) for rows in chunks]
    emit_qkv(0)
    retention(2)
    emit_qkv(1)
    retention(3)
    emit_qkv(2)
    r_i = lax.broadcasted_iota(jnp.int32, (CHUNK, CHUNK), 0)
    c_i = lax.broadcasted_iota(jnp.int32, (CHUNK, CHUNK), 1)
    w_causal = [jnp.where(c_i <= r_i, sgu_w_ref[h], 0.0).astype(BF16) for h in range(N_HEADS)]
    for rows, y in zip(chunks, _sgu_mix(sgu_ops, w_causal, sgu_b_ref)):
        yb_ref[rows, :] = y.astype(BF16)

    lane = lax.broadcasted_iota(jnp.int32, (1, LANES), 1)
    logits = jnp.zeros((tm, LANES), F32)
    for h in range(N_HEADS):
        r = jnp.sum(xf * wf_ref[h:h + 1, :], axis=-1, keepdims=True)
        logits = logits + r * (lane == h).astype(F32)
    flog_ref[...] = jnp.transpose(logits)[:flog_ref.shape[0], :]


def _inmix(h, g, w_main, wf, layer, pool_w, pool_scale, sgu_g, sgu_w, sgu_bias, ret_tables, ret_g, seq, tm):
    n, d = h.shape
    assert seq % tm == 0 and tm % CHUNK == 0
    tiles_per_seq = seq // tm
    cos, sin, hm, zeta, decay, xi, cd, bd = ret_tables

    def layer_resident(shape):
        return pl.BlockSpec((None,) + shape, lambda i: (layer, 0, 0), pipeline_mode=pl.Buffered(1))

    def rows(width):
        return pl.BlockSpec((tm, width), lambda i: (i, 0))

    pos_rows = pl.BlockSpec((tm, LANES), lambda i: (i % tiles_per_seq, 0))
    mix_out = jax.ShapeDtypeStruct((n, GROUP_WIDTH), BF16)
    return pl.pallas_call(
        functools.partial(_inmix_kernel, tiles_per_seq=tiles_per_seq),
        grid=(n // tm,),
        in_specs=[rows(d), _resident((1, d)), layer_resident(w_main.shape[1:]), layer_resident(wf.shape[1:]),
                  _resident(pool_w.shape), _resident((1, GROUP_WIDTH)),
                  _resident((1, GROUP_WIDTH)), _resident(sgu_w.shape), _resident((CHUNK, GROUP_WIDTH)),
                  pos_rows, pos_rows, _resident(hm.shape), _resident(zeta.shape), _resident(decay.shape),
                  _resident(xi.shape), _resident(cd.shape), _resident(bd.shape), _resident((1, GROUP_WIDTH))],
        out_specs=[rows(3 * GROUP_WIDTH), pl.BlockSpec((SUBLANES, tm), lambda i: (0, i)),
                   rows(GROUP_WIDTH), rows(GROUP_WIDTH), rows(GROUP_WIDTH)],
        out_shape=[jax.ShapeDtypeStruct((n, 3 * GROUP_WIDTH), BF16), jax.ShapeDtypeStruct((SUBLANES, n), F32),
                   mix_out, mix_out, mix_out],
        scratch_shapes=[pltpu.VMEM((2 * LANES, GROUP_WIDTH), F32), pltpu.VMEM((POOL_HALO, GROUP_WIDTH), F32)],
        compiler_params=_cparams("arbitrary"),
        name="inmix",
    )(h, g, w_main, wf, pool_w, pool_scale, sgu_g, sgu_w, sgu_bias, cos, sin, hm, zeta, decay, xi, cd, bd, ret_g)


def _fox_kernel(flog_ref, bf_ref, q_ref, k_ref, v_ref, *rest, tq, tk, td, hps, scale, n_cast):
    cast_src, o_ref, cast_dst, c_scr = rest[:n_cast], rest[n_cast], rest[n_cast + 1:2 * n_cast + 1], rest[-1]
    hg = pl.program_id(1)
    qi = pl.program_id(2)
    s_len = k_ref.shape[0]
    log2e = float(np.log2(np.e))

    for src, dst in zip(cast_src, cast_dst):
        dst[...] = src[...].astype(BF16)

    @pl.when((hg == 0) & (qi == 0))
    def _():
        x = flog_ref[...] + bf_ref[:, 0:1]
        c = jnp.minimum(x, 0.0) - jnp.log1p(jnp.exp(-jnp.abs(x)))
        lane = lax.broadcasted_iota(jnp.int32, c.shape, 1)
        sh = 1
        while sh < s_len:
            c = c + jnp.where(lane >= sh, pltpu.roll(c, sh, axis=1), 0.0)
            sh *= 2
        c = c * log2e
        for j in range(s_len // td):
            c_scr[j] = c[:, j * td:(j + 1) * td]

    heads = [slice(h * LANES, (h + 1) * LANES) for h in range(hps)]
    qs = [(q_ref[:, hs].astype(F32) * (scale * log2e)).astype(BF16) for hs in heads]

    def update(state, q, kb, vb, ck, diagonal):
        m, l, acc = state
        s = lax.dot_general(q, kb, (((1,), (1,)), ((), ())), preferred_element_type=F32) - ck
        if diagonal:
            n = q.shape[0]
            own = s[:, s.shape[1] - n:]
            r_i = lax.broadcasted_iota(jnp.int32, own.shape, 0)
            c_i = lax.broadcasted_iota(jnp.int32, own.shape, 1)
            own = jnp.where(c_i <= r_i, own, NEG_BIG)
            s = own if n == s.shape[1] else jnp.concatenate([s[:, :s.shape[1] - n], own], axis=1)
        m_new = jnp.maximum(m, jnp.max(s, axis=-1, keepdims=True))
        alpha = jnp.exp2(m - m_new)
        p = jnp.exp2(s - m_new)
        l = alpha * l + jnp.sum(p, axis=-1, keepdims=True)
        acc = alpha * acc + jnp.dot(p.astype(BF16), vb, preferred_element_type=F32)
        return m_new, l, acc

    def bias_row(h, key_start, width):
        first = key_start // td
        parts = [c_scr[first + i, pl.ds(hg * hps + h, 1), :] for i in range(width // td)]
        return parts[0] if len(parts) == 1 else jnp.concatenate(parts, axis=1)

    def full_block(j, carry):
        start = j * tk
        return tuple(
            update(carry[h], qs[h], k_ref[pl.ds(start, tk), hs], v_ref[pl.ds(start, tk), hs], bias_row(h, start, tk),
                   False)
            for h, hs in enumerate(heads))

    def attend(tile):
        carry = tuple((jnp.full((tq, 1), NEG_BIG, F32), jnp.zeros((tq, 1), F32), jnp.zeros((tq, LANES), F32))
                      for _ in heads)
        for j in range(tile * tq // tk):
            carry = full_block(j, carry)
        start = tile * tq
        for h, hs in enumerate(heads):
            m, l, acc = carry[h]
            for r in range(tq // td):
                rows = slice(r * td, (r + 1) * td)
                n_keys = (r + 1) * td
                _, l_r, acc_r = update((m[rows], l[rows], acc[rows]), qs[h][rows], k_ref[pl.ds(start, n_keys), hs],
                                       v_ref[pl.ds(start, n_keys), hs], bias_row(h, start, n_keys), True)
                o_ref[rows, hs] = (acc_r / l_r).astype(BF16)

    for tile in range(s_len // tq):
        pl.when(qi == tile)(functools.partial(attend, tile))


def _fox(qkv, flog_t, bf_tab, batch, seq, tq, tk, td, hps, cast_weights, layer):
    assert tq % td == 0 and tk % td == 0 and tq % tk == 0
    nq = seq // tq
    width = hps * LANES
    per_piece = GROUP_WIDTH // width
    n_steps = batch * per_piece * nq

    def step(b, g, i):
        return (b * per_piece + g) * nq + i

    cast_in, cast_out, cast_shapes = [], [], []
    for w in cast_weights:
        _, rows, cols = w.shape
        slab = rows // n_steps
        assert slab * n_steps == rows and slab % BF16_TILE_ROWS == 0, (w.shape, n_steps)
        cast_in.append(pl.BlockSpec((None, slab, cols), lambda b, g, i: (layer, step(b, g, i), 0)))
        cast_out.append(pl.BlockSpec((slab, cols), lambda b, g, i: (step(b, g, i), 0)))
        cast_shapes.append(jax.ShapeDtypeStruct((rows, cols), BF16))

    outs = pl.pallas_call(
        functools.partial(_fox_kernel, tq=tq, tk=tk, td=td, hps=hps, scale=LANES ** -0.5, n_cast=len(cast_weights)),
        grid=(batch, per_piece, nq),
        in_specs=[
            pl.BlockSpec((SUBLANES, seq), lambda b, g, i: (0, b)),
            _resident((SUBLANES, LANES)),
            pl.BlockSpec((tq, width), lambda b, g, i: (b * nq + i, g)),
            pl.BlockSpec((seq, width), lambda b, g, i: (b, per_piece + g)),
            pl.BlockSpec((seq, width), lambda b, g, i: (b, 2 * per_piece + g)),
        ] + cast_in,
        out_specs=[pl.BlockSpec((tq, width), lambda b, g, i: (b * nq + i, g))] + cast_out,
        out_shape=[jax.ShapeDtypeStruct((batch * seq, GROUP_WIDTH), BF16)] + cast_shapes,
        scratch_shapes=[pltpu.VMEM((seq // td, SUBLANES, td), F32)],
        compiler_params=_cparams("parallel", "arbitrary", "arbitrary"),
        name="fox",
    )(flog_t, bf_tab, qkv, qkv, qkv, *cast_weights)
    return outs[0], outs[1:]


def _ret_tables(seq):
    half = RET_QK // 2
    inv = np.exp(-(np.arange(half, dtype=np.float32) / half) * np.float32(np.log(ROPE_BASE))).astype(np.float32)
    ang = np.arange(seq, dtype=np.float32)[:, None] * inv[None, :]
    cos = np.tile(np.cos(ang).astype(np.float32), (1, N_HEADS))
    sin = np.tile(np.sin(ang).astype(np.float32), (1, N_HEADS))
    gamma = (1.0 - np.exp((-5.0 - np.arange(N_HEADS, dtype=np.float32)) * np.float32(np.log(2.0)))).astype(np.float32)
    log_gamma = np.log(gamma).astype(np.float32)
    l = np.arange(CHUNK, dtype=np.float32)
    diff = l[:, None] - l[None, :]
    decay = np.where(diff[None] >= 0, np.exp(np.maximum(diff, 0.0)[None] * log_gamma[:, None, None]), 0.0)
    decay = np.transpose(decay, (1, 0, 2)).reshape(CHUNK, N_HEADS * CHUNK).astype(np.float32)
    xi = np.repeat(np.exp((l + 1.0)[:, None] * log_gamma[None, :]), LANES, axis=1).astype(np.float32)
    lane_head = np.tile(np.repeat(np.arange(N_HEADS), half), 2)
    zeta = np.exp((CHUNK - 1.0 - l)[:, None] * log_gamma[lane_head][None, :]).astype(np.float32)
    hm = np.zeros((SUBLANES, 2 * LANES), np.float32)
    for h in range(N_HEADS):
        hm[h] = lane_head == h
    col_head = np.repeat(np.arange(N_HEADS), LANES)
    cd = np.exp(CHUNK * log_gamma)[col_head][None, :].astype(np.float32)
    bd = (lane_head[:, None] == col_head[None, :]).astype(np.float32)
    return tuple(jnp.asarray(t) for t in (cos, sin, hm, zeta, decay, xi, cd, bd))


def _outproj_kernel(h_ref, ya_ref, yb_ref, yc_ref, yd_ref, w_ref, o_ref, mix_scr):
    for gi, y_ref in enumerate((ya_ref, yb_ref, yc_ref, yd_ref)):
        mix_scr[:, gi * GROUP_WIDTH:(gi + 1) * GROUP_WIDTH] = y_ref[...]
    mix = mix_scr[...]
    for c in range(o_ref.shape[1] // GROUP_WIDTH):
        cols = slice(c * GROUP_WIDTH, (c + 1) * GROUP_WIDTH)
        o_ref[:, cols] = h_ref[:, cols] + jnp.dot(mix, w_ref[:, cols], preferred_element_type=F32)


def _outproj(h, ys, w_out, tm):
    n, d = h.shape
    y_spec = pl.BlockSpec((tm, GROUP_WIDTH), lambda i: (i, 0))
    w_spec = _resident(w_out.shape)
    return pl.pallas_call(
        _outproj_kernel,
        grid=(n // tm,),
        in_specs=[pl.BlockSpec((tm, d), lambda i: (i, 0)), y_spec, y_spec, y_spec, y_spec, w_spec],
        out_specs=pl.BlockSpec((tm, d), lambda i: (i, 0)),
        out_shape=jax.ShapeDtypeStruct((n, d), F32),
        scratch_shapes=[pltpu.VMEM((tm, N_HEADS * GROUP_WIDTH), BF16)],
        compiler_params=_cparams("parallel"),
        name="outproj",
    )(h, *ys, w_out)


def _mlp_kernel(h_ref, g_ref, w1_hbm, w2_hbm, gf_ref, o_ref, hn_scr, w1_buf, w2_buf, sems, *,
                final_norm, n_split, tf, n_blocks):
    i = pl.program_id(0)
    sub = h_ref.shape[0] // n_split
    row_groups = [slice(r * sub, (r + 1) * sub) for r in range(n_split)]

    def block_copies(b, slot):
        cols = pl.ds(pl.multiple_of(b * tf, tf), tf)
        return (pltpu.make_async_copy(w1_hbm.at[:, cols], w1_buf.at[slot], sems.at[0, slot]),
                pltpu.make_async_copy(w2_hbm.at[cols, :], w2_buf.at[slot], sems.at[1, slot]))

    def start(b, slot):
        for c in block_copies(b, slot):
            c.start()

    @pl.when(i == 0)
    def _():
        start(0, 0)

    for rows in row_groups:
        x = h_ref[rows, :]
        hn_scr[rows, :] = (x * _rms_scale(x) * g_ref[...]).astype(BF16)
        o_ref[rows, :] = x

    def block(b, carry):
        slot = b % 2
        for c in block_copies(b, slot):
            c.wait()

        @pl.when(b + 1 < n_blocks)
        def _():
            start(b + 1, 1 - slot)

        @pl.when((b + 1 == n_blocks) & (i + 1 < pl.num_programs(0)))
        def _():
            start(0, 1 - slot)

        for rows in row_groups:
            a = jnp.dot(hn_scr[rows, :], w1_buf[slot], preferred_element_type=F32)
            a = jnp.square(jnp.maximum(a, 0.0)).astype(BF16)
            o_ref[rows, :] += jnp.dot(a, w2_buf[slot], preferred_element_type=F32)
        return carry

    lax.fori_loop(0, n_blocks, block, 0)

    if final_norm:
        for rows in row_groups:
            y = o_ref[rows, :]
            o_ref[rows, :] = y * _rms_scale(y) * gf_ref[...]


def _mlp(h, g, w1, w2, g_final, tm, tf, final_norm):
    n, d = h.shape
    d_ff = w1.shape[1]
    n_blocks = d_ff // tf
    assert n_blocks * tf == d_ff and n_blocks % 2 == 0
    return pl.pallas_call(
        functools.partial(_mlp_kernel, final_norm=final_norm, n_split=max(1, tm // ROW_TILE), tf=tf,
                          n_blocks=n_blocks),
        grid=(n // tm,),
        in_specs=[
            pl.BlockSpec((tm, d), lambda i: (i, 0)),
            _resident((1, d)),
            pl.BlockSpec(memory_space=pl.ANY),
            pl.BlockSpec(memory_space=pl.ANY),
            _resident((1, d)),
        ],
        out_specs=pl.BlockSpec((tm, d), lambda i: (i, 0)),
        out_shape=jax.ShapeDtypeStruct((n, d), F32),
        scratch_shapes=[pltpu.VMEM((tm, d), BF16), pltpu.VMEM((2, d, tf), BF16), pltpu.VMEM((2, tf, d), BF16),
                        pltpu.SemaphoreType.DMA((2, 2))],
        compiler_params=_cparams("arbitrary"),
        name="mlp",
    )(h, g, w1, w2, g_final)


def kernel(x, norm_mix_g, w_in, fox_b_f, pool_w, pool_scale, sgu_norm_g, sgu_w_s, sgu_b, ret_norm_g, w_out,
           norm_mlp_g, w_ff1, w_ff2, norm_final_g):
    batch, seq, d = x.shape
    depth = w_in.shape[0]
    n = batch * seq
    tm_in = min(ROW_TILE, seq)
    tm_out = tm_mlp = min(WIDE_ROW_TILE, n)
    tf_mlp = FF_TILE
    fox_tiles = (min(FOX_TQ, seq), min(FOX_TK, seq), min(FOX_TD, seq), FOX_HEADS_PER_STEP)
    tables = _ret_tables(seq)

    w_main, w_forget = _wprep(w_in)

    h = x.reshape(n, d)
    g_final = norm_final_g.reshape(1, d)
    for layer in range(depth):
        sgu_bias = jnp.repeat(jnp.transpose(sgu_b[layer]), LANES, axis=1)
        qkv, flog_t, y_a, y_b, y_d = _inmix(
            h, norm_mix_g[layer].reshape(1, d), w_main, w_forget, layer,
            pool_w[layer], pool_scale[layer].reshape(1, GROUP_WIDTH),
            sgu_norm_g[layer].reshape(1, GROUP_WIDTH), sgu_w_s[layer], sgu_bias,
            tables, ret_norm_g[layer].reshape(1, GROUP_WIDTH), seq, tm_in)
        bf_tab = jnp.zeros((SUBLANES, LANES), F32).at[:N_HEADS].set(jnp.broadcast_to(fox_b_f[layer][:, None], (N_HEADS, LANES)))
        y_c, (w_out_b, w_ff1_b, w_ff2_b) = _fox(qkv, flog_t, bf_tab, batch, seq, *fox_tiles,
                                                cast_weights=(w_out, w_ff1, w_ff2), layer=layer)

        h = _outproj(h, (y_a, y_b, y_c, y_d), w_out_b, tm_out)
        h = _mlp(h, norm_mlp_g[layer].reshape(1, d), w_ff1_b, w_ff2_b, g_final, tm_mlp, tf_mlp,
                 final_norm=(layer == depth - 1))
    return h.reshape(batch, seq, d)
```

```python
import functools

import numpy as np
import jax
import jax.numpy as jnp
from jax import lax
from jax.experimental import pallas as pl
from jax.experimental.pallas import tpu as pltpu

F32 = jnp.float32
BF16 = jnp.bfloat16

EPS = 1e-6
NEG_BIG = -1e30
GROUP_WIDTH = 512
LANES = 128
SUBLANES = 8
BF16_TILE_ROWS = 16
ROW_TILE = 512
WIDE_ROW_TILE = 1024
FF_TILE = 1024
N_HEADS = 4
POOL_WINDOWS = (2, 4, 8, 16)
CHUNK = 128
POOL_HALO = 16
RET_QK = 64
ROPE_BASE = 10000.0
N_MAIN = 9 * GROUP_WIDTH
F_COL0 = 6 * GROUP_WIDTH
W_PREP_COLS = 256
V7X_VMEM_LIMIT = 56 * 1024 * 1024
FOX_TQ, FOX_TK, FOX_TD, FOX_HEADS_PER_STEP = ROW_TILE, ROW_TILE, ROW_TILE, N_HEADS


def _cparams(*sem):
    return pltpu.CompilerParams(dimension_semantics=sem, vmem_limit_bytes=V7X_VMEM_LIMIT)


def _resident(shape):
    nd = len(shape)
    return pl.BlockSpec(shape, lambda *_: (0,) * nd, pipeline_mode=pl.Buffered(1))


def _rms_scale(x):
    return lax.rsqrt(jnp.mean(x * x, axis=-1, keepdims=True) + EPS)


def _wprep_kernel(w_ref, f_ref, w_out_ref, f_out_ref):
    t = pl.program_id(0)
    n_layers = w_ref.shape[1]
    half = RET_QK // 2

    def emit(row_groups):
        for layer in range(n_layers):
            for b, ranges in enumerate(row_groups):
                x = jnp.concatenate([w_ref[r0:r1, layer, :] for r0, r1 in ranges], axis=0)
                w_out_ref[layer, :, b * LANES:(b + 1) * LANES] = jnp.transpose(x).astype(BF16)

    n_blocks = W_PREP_COLS // LANES
    retention_qk = (t >= F_COL0 // W_PREP_COLS) & (t < (F_COL0 + 2 * N_HEADS * RET_QK) // W_PREP_COLS)

    @pl.when(jnp.logical_not(retention_qk))
    def _():
        emit([[(b * LANES, (b + 1) * LANES)] for b in range(n_blocks)])

    @pl.when(retention_qk)
    def _():
        emit([[(hd * RET_QK + hf * half, hd * RET_QK + (hf + 1) * half) for hd in range(N_HEADS)]
              for hf in range(n_blocks)])

    keep = lax.broadcasted_iota(jnp.int32, f_out_ref.shape[1:], 0) < N_HEADS
    for layer in range(n_layers):
        f_out_ref[layer] = jnp.where(keep, f_ref[:, layer, :], 0.0)


def _wprep(w_in):
    n_layers, d, _ = w_in.shape
    assert W_PREP_COLS == N_HEADS * RET_QK and F_COL0 % W_PREP_COLS == 0
    w_t = jnp.transpose(w_in, (2, 0, 1))

    def src_row(t):
        return t * W_PREP_COLS + jnp.where(t >= F_COL0 // W_PREP_COLS, N_HEADS, 0)

    return pl.pallas_call(
        _wprep_kernel,
        grid=(N_MAIN // W_PREP_COLS,),
        in_specs=[
            pl.BlockSpec((pl.Element(W_PREP_COLS), pl.Element(n_layers), pl.Element(d)), lambda t: (src_row(t), 0, 0)),
            pl.BlockSpec((pl.Element(SUBLANES), pl.Element(n_layers), pl.Element(d)), lambda t: (F_COL0, 0, 0)),
        ],
        out_specs=[
            pl.BlockSpec((n_layers, d, W_PREP_COLS), lambda t: (0, 0, t)),
            pl.BlockSpec((n_layers, SUBLANES, d), lambda t: (0, 0, 0)),
        ],
        out_shape=[jax.ShapeDtypeStruct((n_layers, d, N_MAIN), BF16),
                   jax.ShapeDtypeStruct((n_layers, SUBLANES, d), F32)],
        compiler_params=_cparams("arbitrary"),
        name="wprep",
    )(w_t, w_t)


def _head_norm(x):
    mu = jnp.mean(x, axis=-1, keepdims=True)
    d = x - mu
    var = jnp.mean(d * d, axis=-1, keepdims=True)
    return d * lax.rsqrt(var + EPS)


def _pool_windows(a, halo, pos0):
    tm = a.shape[0]
    ext = jnp.concatenate([halo, a], axis=0)
    pos = lax.broadcasted_iota(jnp.int32, (tm, LANES), 0) + pos0
    ps = []
    for gi, w in enumerate(POOL_WINDOWS):
        cols = slice(gi * LANES, (gi + 1) * LANES)
        tot = ext[:, cols]
        sh = 1
        while sh < w:
            tot = tot + pltpu.roll(tot, sh, axis=0)
            sh *= 2
        cnt = jnp.minimum(pos + 1, w).astype(F32)
        ps.append((tot[POOL_HALO:] / cnt - a[:, cols]).astype(BF16))
    return ps


def _pool_mix(ps, pw_ref, ps_ref):
    ys = [jnp.dot(p, pw_ref[gi].astype(BF16), preferred_element_type=F32) for gi, p in enumerate(ps)]
    return jnp.concatenate(ys, axis=1) * ps_ref[...]


def _sgu_gates(u, v, ng_ref):
    v = jax.nn.gelu(v)
    vns = [(_head_norm(v[:, h * LANES:(h + 1) * LANES]) * ng_ref[:, h * LANES:(h + 1) * LANES]).astype(BF16)
           for h in range(N_HEADS)]
    return jax.nn.gelu(u), vns


def _sgu_mix(gates, w_causal, bias_ref):
    zs = [jnp.dot(w_causal[h], jnp.concatenate([vns[h] for _, vns in gates], axis=1), preferred_element_type=F32)
          for h in range(N_HEADS)]
    outs = []
    for ci, (gu, _) in enumerate(gates):
        z = jnp.concatenate([zs[h][:, ci * LANES:(ci + 1) * LANES] for h in range(N_HEADS)], axis=1)
        outs.append(gu * (z + bias_ref[...]))
    return outs


def _rope(x, cos, sin):
    x1, x2 = x[:, :LANES], x[:, LANES:]
    return jnp.concatenate([x1 * cos - x2 * sin, x1 * sin + x2 * cos], axis=1)


def _retention_scores(q, k, cos, sin, hm_ref, decay_ref, zeta_ref):
    qb = _rope(q, cos, sin).astype(BF16)
    kr = _rope(k, cos, sin) * (RET_QK ** -0.5)
    k_heads = jnp.concatenate([(kr * hm_ref[h:h + 1, :]).astype(BF16) for h in range(N_HEADS)], axis=0)
    s = lax.dot_general(qb, k_heads, (((1,), (1,)), ((), ())), preferred_element_type=F32)
    kz_t = jnp.transpose(kr * zeta_ref[...]).astype(BF16)
    return qb, (s * decay_ref[...]).astype(BF16), kz_t


def _retention_mix(qb, sb, kz_t, v, g, state_scr, xi_ref, cd_ref, bd_ref, ng_ref):
    y_intra = jnp.concatenate(
        [jnp.dot(sb[:, h * LANES:(h + 1) * LANES], v[:, h * LANES:(h + 1) * LANES], preferred_element_type=F32)
         for h in range(N_HEADS)], axis=1)
    state = state_scr[...]
    y_cross = jnp.dot(qb, state.astype(BF16), preferred_element_type=F32) * xi_ref[...]
    upd = jnp.dot(kz_t, v, preferred_element_type=F32)
    state_scr[...] = state * cd_ref[...] + upd * bd_ref[...]
    y = y_intra + y_cross
    yn = jnp.concatenate([_head_norm(y[:, h * LANES:(h + 1) * LANES]) for h in range(N_HEADS)], axis=1)
    return jax.nn.silu(g) * (yn * ng_ref[...])


def _inmix_kernel(h_ref, g_ref, w_ref, wf_ref, pool_w_ref, pool_s_ref, sgu_g_ref, sgu_w_ref, sgu_b_ref,
                  cos_ref, sin_ref, hm_ref, zeta_ref, decay_ref, xi_ref, cd_ref, bd_ref, ret_g_ref,
                  qkv_ref, flog_ref, ya_ref, yb_ref, yd_ref, state_scr, halo_scr, *, tiles_per_seq):
    tm = h_ref.shape[0]
    tile_in_seq = pl.program_id(0) % tiles_per_seq

    @pl.when(tile_in_seq == 0)
    def _():
        state_scr[...] = jnp.zeros_like(state_scr)
        halo_scr[...] = jnp.zeros_like(halo_scr)

    x = h_ref[...]
    xf = x * _rms_scale(x) * g_ref[...]
    xn = xf.astype(BF16)

    def project(piece):
        cols = slice(piece * GROUP_WIDTH, (piece + 1) * GROUP_WIDTH)
        return jnp.dot(xn, w_ref[:, cols], preferred_element_type=F32)

    def emit_qkv(piece):
        qkv_ref[:, piece * GROUP_WIDTH:(piece + 1) * GROUP_WIDTH] = project(3 + piece).astype(BF16)

    chunks = [slice(ci * CHUNK, (ci + 1) * CHUNK) for ci in range(tm // CHUNK)]
    assert len(chunks) == 4

    a = project(0)
    pool_ops = _pool_windows(a, halo_scr[...], tile_in_seq * tm)
    halo_scr[...] = a[tm - POOL_HALO:, :]
    qk = project(6)
    ret_ops = [_retention_scores(qk[rows, :2 * LANES], qk[rows, 2 * LANES:], cos_ref[rows, :], sin_ref[rows, :],
                                 hm_ref, decay_ref, zeta_ref) for rows in chunks]
    rv, rg = project(7).astype(BF16), project(8)
    ya_ref[...] = _pool_mix(pool_ops, pool_w_ref, pool_s_ref).astype(BF16)

    def retention(ci):
        rows = chunks[ci]
        yd_ref[rows, :] = _retention_mix(*ret_ops[ci], rv[rows], rg[rows], state_scr, xi_ref, cd_ref, bd_ref,
                                         ret_g_ref).astype(BF16)

    u = project(1)
    retention(0)
    v = project(2)
    retention(1)
    sgu_ops = [_sgu_gates(u[rows], v[rows], sgu_g_ref) for rows in chunks]
    emit_qkv(0)
    retention(2)
    emit_qkv(1)
    retention(3)
    emit_qkv(2)
    r_i = lax.broadcasted_iota(jnp.int32, (CHUNK, CHUNK), 0)
    c_i = lax.broadcasted_iota(jnp.int32, (CHUNK, CHUNK), 1)
    w_causal = [jnp.where(c_i <= r_i, sgu_w_ref[h], 0.0).astype(BF16) for h in range(N_HEADS)]
    for rows, y in zip(chunks, _sgu_mix(sgu_ops, w_causal, sgu_b_ref)):
        yb_ref[rows, :] = y.astype(BF16)

    lane = lax.broadcasted_iota(jnp.int32, (1, LANES), 1)
    logits = jnp.zeros((tm, LANES), F32)
    for h in range(N_HEADS):
        r = jnp.sum(xf * wf_ref[h:h + 1, :], axis=-1, keepdims=True)
        logits = logits + r * (lane == h).astype(F32)
    flog_ref[...] = jnp.transpose(logits)[:flog_ref.shape[0], :]


def _inmix(h, g, w_main, wf, layer, pool_w, pool_scale, sgu_g, sgu_w, sgu_bias, ret_tables, ret_g, seq, tm):
    n, d = h.shape
    assert seq % tm == 0 and tm % CHUNK == 0
    tiles_per_seq = seq // tm
    cos, sin, hm, zeta, decay, xi, cd, bd = ret_tables

    def layer_resident(shape):
        return pl.BlockSpec((None,) + shape, lambda i: (layer, 0, 0), pipeline_mode=pl.Buffered(1))

    def rows(width):
        return pl.BlockSpec((tm, width), lambda i: (i, 0))

    pos_rows = pl.BlockSpec((tm, LANES), lambda i: (i % tiles_per_seq, 0))
    mix_out = jax.ShapeDtypeStruct((n, GROUP_WIDTH), BF16)
    return pl.pallas_call(
        functools.partial(_inmix_kernel, tiles_per_seq=tiles_per_seq),
        grid=(n // tm,),
        in_specs=[rows(d), _resident((1, d)), layer_resident(w_main.shape[1:]), layer_resident(wf.shape[1:]),
                  _resident(pool_w.shape), _resident((1, GROUP_WIDTH)),
                  _resident((1, GROUP_WIDTH)), _resident(sgu_w.shape), _resident((CHUNK, GROUP_WIDTH)),
                  pos_rows, pos_rows, _resident(hm.shape), _resident(zeta.shape), _resident(decay.shape),
                  _resident(xi.shape), _resident(cd.shape), _resident(bd.shape), _resident((1, GROUP_WIDTH))],
        out_specs=[rows(3 * GROUP_WIDTH), pl.BlockSpec((SUBLANES, tm), lambda i: (0, i)),
                   rows(GROUP_WIDTH), rows(GROUP_WIDTH), rows(GROUP_WIDTH)],
        out_shape=[jax.ShapeDtypeStruct((n, 3 * GROUP_WIDTH), BF16), jax.ShapeDtypeStruct((SUBLANES, n), F32),
                   mix_out, mix_out, mix_out],
        scratch_shapes=[pltpu.VMEM((2 * LANES, GROUP_WIDTH), F32), pltpu.VMEM((POOL_HALO, GROUP_WIDTH), F32)],
        compiler_params=_cparams("arbitrary"),
        name="inmix",
    )(h, g, w_main, wf, pool_w, pool_scale, sgu_g, sgu_w, sgu_bias, cos, sin, hm, zeta, decay, xi, cd, bd, ret_g)


def _fox_kernel(flog_ref, bf_ref, q_ref, k_ref, v_ref, *rest, tq, tk, td, hps, scale, n_cast):
    cast_src, o_ref, cast_dst, c_scr = rest[:n_cast], rest[n_cast], rest[n_cast + 1:2 * n_cast + 1], rest[-1]
    hg = pl.program_id(1)
    qi = pl.program_id(2)
    s_len = k_ref.shape[0]
    log2e = float(np.log2(np.e))

    for src, dst in zip(cast_src, cast_dst):
        dst[...] = src[...].astype(BF16)

    @pl.when((hg == 0) & (qi == 0))
    def _():
        x = flog_ref[...] + bf_ref[:, 0:1]
        c = jnp.minimum(x, 0.0) - jnp.log1p(jnp.exp(-jnp.abs(x)))
        lane = lax.broadcasted_iota(jnp.int32, c.shape, 1)
        sh = 1
        while sh < s_len:
            c = c + jnp.where(lane >= sh, pltpu.roll(c, sh, axis=1), 0.0)
            sh *= 2
        c = c * log2e
        for j in range(s_len // td):
            c_scr[j] = c[:, j * td:(j + 1) * td]

    heads = [slice(h * LANES, (h + 1) * LANES) for h in range(hps)]
    qs = [(q_ref[:, hs].astype(F32) * (scale * log2e)).astype(BF16) for hs in heads]

    def update(state, q, kb, vb, ck, diagonal):
        m, l, acc = state
        s = lax.dot_general(q, kb, (((1,), (1,)), ((), ())), preferred_element_type=F32) - ck
        if diagonal:
            n = q.shape[0]
            own = s[:, s.shape[1] - n:]
            r_i = lax.broadcasted_iota(jnp.int32, own.shape, 0)
            c_i = lax.broadcasted_iota(jnp.int32, own.shape, 1)
            own = jnp.where(c_i <= r_i, own, NEG_BIG)
            s = own if n == s.shape[1] else jnp.concatenate([s[:, :s.shape[1] - n], own], axis=1)
        m_new = jnp.maximum(m, jnp.max(s, axis=-1, keepdims=True))
        alpha = jnp.exp2(m - m_new)
        p = jnp.exp2(s - m_new)
        l = alpha * l + jnp.sum(p, axis=-1, keepdims=True)
        acc = alpha * acc + jnp.dot(p.astype(BF16), vb, preferred_element_type=F32)
        return m_new, l, acc

    def bias_row(h, key_start, width):
        first = key_start // td
        parts = [c_scr[first + i, pl.ds(hg * hps + h, 1), :] for i in range(width // td)]
        return parts[0] if len(parts) == 1 else jnp.concatenate(parts, axis=1)

    def full_block(j, carry):
        start = j * tk
        return tuple(
            update(carry[h], qs[h], k_ref[pl.ds(start, tk), hs], v_ref[pl.ds(start, tk), hs], bias_row(h, start, tk),
                   False)
            for h, hs in enumerate(heads))

    def attend(tile):
        carry = tuple((jnp.full((tq, 1), NEG_BIG, F32), jnp.zeros((tq, 1), F32), jnp.zeros((tq, LANES), F32))
                      for _ in heads)
        for j in range(tile * tq // tk):
            carry = full_block(j, carry)
        start = tile * tq
        for h, hs in enumerate(heads):
            m, l, acc = carry[h]
            for r in range(tq // td):
                rows = slice(r * td, (r + 1) * td)
                n_keys = (r + 1) * td
                _, l_r, acc_r = update((m[rows], l[rows], acc[rows]), qs[h][rows], k_ref[pl.ds(start, n_keys), hs],
                                       v_ref[pl.ds(start, n_keys), hs], bias_row(h, start, n_keys), True)
                o_ref[rows, hs] = (acc_r / l_r).astype(BF16)

    for tile in range(s_len // tq):
        pl.when(qi == tile)(functools.partial(attend, tile))


def _fox(qkv, flog_t, bf_tab, batch, seq, tq, tk, td, hps, cast_weights, layer):
    assert tq % td == 0 and tk % td == 0 and tq % tk == 0
    nq = seq // tq
    width = hps * LANES
    per_piece = GROUP_WIDTH // width
    n_steps = batch * per_piece * nq

    def step(b, g, i):
        return (b * per_piece + g) * nq + i

    cast_in, cast_out, cast_shapes = [], [], []
    for w in cast_weights:
        _, rows, cols = w.shape
        slab = rows // n_steps
        assert slab * n_steps == rows and slab % BF16_TILE_ROWS == 0, (w.shape, n_steps)
        cast_in.append(pl.BlockSpec((None, slab, cols), lambda b, g, i: (layer, step(b, g, i), 0)))
        cast_out.append(pl.BlockSpec((slab, cols), lambda b, g, i: (step(b, g, i), 0)))
        cast_shapes.append(jax.ShapeDtypeStruct((rows, cols), BF16))

    outs = pl.pallas_call(
        functools.partial(_fox_kernel, tq=tq, tk=tk, td=td, hps=hps, scale=LANES ** -0.5, n_cast=len(cast_weights)),
        grid=(batch, per_piece, nq),
        in_specs=[
            pl.BlockSpec((SUBLANES, seq), lambda b, g, i: (0, b)),
            _resident((SUBLANES, LANES)),
            pl.BlockSpec((tq, width), lambda b, g, i: (b * nq + i, g)),
            pl.BlockSpec((seq, width), lambda b, g, i: (b, per_piece + g)),
            pl.BlockSpec((seq, width), lambda b, g, i: (b, 2 * per_piece + g)),
        ] + cast_in,
        out_specs=[pl.BlockSpec((tq, width), lambda b, g, i: (b * nq + i, g))] + cast_out,
        out_shape=[jax.ShapeDtypeStruct((batch * seq, GROUP_WIDTH), BF16)] + cast_shapes,
        scratch_shapes=[pltpu.VMEM((seq // td, SUBLANES, td), F32)],
        compiler_params=_cparams("parallel", "arbitrary", "arbitrary"),
        name="fox",
    )(flog_t, bf_tab, qkv, qkv, qkv, *cast_weights)
    return outs[0], outs[1:]


def _ret_tables(seq):
    half = RET_QK // 2
    inv = np.exp(-(np.arange(half, dtype=np.float32) / half) * np.float32(np.log(ROPE_BASE))).astype(np.float32)
    ang = np.arange(seq, dtype=np.float32)[:, None] * inv[None, :]
    cos = np.tile(np.cos(ang).astype(np.float32), (1, N_HEADS))
    sin = np.tile(np.sin(ang).astype(np.float32), (1, N_HEADS))
    gamma = (1.0 - np.exp((-5.0 - np.arange(N_HEADS, dtype=np.float32)) * np.float32(np.log(2.0)))).astype(np.float32)
    log_gamma = np.log(gamma).astype(np.float32)
    l = np.arange(CHUNK, dtype=np.float32)
    diff = l[:, None] - l[None, :]
    decay = np.where(diff[None] >= 0, np.exp(np.maximum(diff, 0.0)[None] * log_gamma[:, None, None]), 0.0)
    decay = np.transpose(decay, (1, 0, 2)).reshape(CHUNK, N_HEADS * CHUNK).astype(np.float32)
    xi = np.repeat(np.exp((l + 1.0)[:, None] * log_gamma[None, :]), LANES, axis=1).astype(np.float32)
    lane_head = np.tile(np.repeat(np.arange(N_HEADS), half), 2)
    zeta = np.exp((CHUNK - 1.0 - l)[:, None] * log_gamma[lane_head][None, :]).astype(np.float32)
    hm = np.zeros((SUBLANES, 2 * LANES), np.float32)
    for h in range(N_HEADS):
        hm[h] = lane_head == h
    col_head = np.repeat(np.arange(N_HEADS), LANES)
    cd = np.exp(CHUNK * log_gamma)[col_head][None, :].astype(np.float32)
    bd = (lane_head[:, None] == col_head[None, :]).astype(np.float32)
    return tuple(jnp.asarray(t) for t in (cos, sin, hm, zeta, decay, xi, cd, bd))


def _outproj_kernel(h_ref, ya_ref, yb_ref, yc_ref, yd_ref, w_ref, o_ref, mix_scr):
    for gi, y_ref in enumerate((ya_ref, yb_ref, yc_ref, yd_ref)):
        mix_scr[:, gi * GROUP_WIDTH:(gi + 1) * GROUP_WIDTH] = y_ref[...]
    mix = mix_scr[...]
    for c in range(o_ref.shape[1] // GROUP_WIDTH):
        cols = slice(c * GROUP_WIDTH, (c + 1) * GROUP_WIDTH)
        o_ref[:, cols] = h_ref[:, cols] + jnp.dot(mix, w_ref[:, cols], preferred_element_type=F32)


def _outproj(h, ys, w_out, tm):
    n, d = h.shape
    y_spec = pl.BlockSpec((tm, GROUP_WIDTH), lambda i: (i, 0))
    w_spec = _resident(w_out.shape)
    return pl.pallas_call(
        _outproj_kernel,
        grid=(n // tm,),
        in_specs=[pl.BlockSpec((tm, d), lambda i: (i, 0)), y_spec, y_spec, y_spec, y_spec, w_spec],
        out_specs=pl.BlockSpec((tm, d), lambda i: (i, 0)),
        out_shape=jax.ShapeDtypeStruct((n, d), F32),
        scratch_shapes=[pltpu.VMEM((tm, N_HEADS * GROUP_WIDTH), BF16)],
        compiler_params=_cparams("parallel"),
        name="outproj",
    )(h, *ys, w_out)


def _mlp_kernel(h_ref, g_ref, w1_hbm, w2_hbm, gf_ref, o_ref, hn_scr, w1_buf, w2_buf, sems, *,
                final_norm, n_split, tf, n_blocks):
    i = pl.program_id(0)
    sub = h_ref.shape[0] // n_split
    row_groups = [slice(r * sub, (r + 1) * sub) for r in range(n_split)]

    def block_copies(b, slot):
        cols = pl.ds(pl.multiple_of(b * tf, tf), tf)
        return (pltpu.make_async_copy(w1_hbm.at[:, cols], w1_buf.at[slot], sems.at[0, slot]),
                pltpu.make_async_copy(w2_hbm.at[cols, :], w2_buf.at[slot], sems.at[1, slot]))

    def start(b, slot):
        for c in block_copies(b, slot):
            c.start()

    @pl.when(i == 0)
    def _():
        start(0, 0)

    for rows in row_groups:
        x = h_ref[rows, :]
        hn_scr[rows, :] = (x * _rms_scale(x) * g_ref[...]).astype(BF16)
        o_ref[rows, :] = x

    def block(b, slot):
        for c in block_copies(b, slot):
            c.wait()

        @pl.when(b + 1 < n_blocks)
        def _():
            start(b + 1, 1 - slot)

        @pl.when((b + 1 == n_blocks) & (i + 1 < pl.num_programs(0)))
        def _():
            start(0, 1 - slot)

        for rows in row_groups:
            a = jnp.dot(hn_scr[rows, :], w1_buf[slot], preferred_element_type=F32)
            a = jnp.square(jnp.maximum(a, 0.0)).astype(BF16)
            o_ref[rows, :] += jnp.dot(a, w2_buf[slot], preferred_element_type=F32)

    def block_pair(p, carry):
        block(2 * p, 0)
        block(2 * p + 1, 1)
        return carry

    lax.fori_loop(0, n_blocks // 2, block_pair, 0)

    if final_norm:
        for rows in row_groups:
            y = o_ref[rows, :]
            o_ref[rows, :] = y * _rms_scale(y) * gf_ref[...]


def _mlp(h, g, w1, w2, g_final, tm, tf, final_norm):
    n, d = h.shape
    d_ff = w1.shape[1]
    n_blocks = d_ff // tf
    assert n_blocks * tf == d_ff and n_blocks % 2 == 0
    return pl.pallas_call(
        functools.partial(_mlp_kernel, final_norm=final_norm, n_split=max(1, tm // ROW_TILE), tf=tf,
                          n_blocks=n_blocks),
        grid=(n // tm,),
        in_specs=[
            pl.BlockSpec((tm, d), lambda i: (i, 0)),
            _resident((1, d)),
            pl.BlockSpec(memory_space=pl.ANY),
            pl.BlockSpec(memory_space=pl.ANY),
            _resident((1, d)),
        ],
        out_specs=pl.BlockSpec((tm, d), lambda i: (i, 0)),
        out_shape=jax.ShapeDtypeStruct((n, d), F32),
        scratch_shapes=[pltpu.VMEM((tm, d), BF16), pltpu.VMEM((2, d, tf), BF16), pltpu.VMEM((2, tf, d), BF16),
                        pltpu.SemaphoreType.DMA((2, 2))],
        compiler_params=_cparams("arbitrary"),
        name="mlp",
    )(h, g, w1, w2, g_final)


def kernel(x, norm_mix_g, w_in, fox_b_f, pool_w, pool_scale, sgu_norm_g, sgu_w_s, sgu_b, ret_norm_g, w_out,
           norm_mlp_g, w_ff1, w_ff2, norm_final_g):
    batch, seq, d = x.shape
    depth = w_in.shape[0]
    n = batch * seq
    tm_in = min(ROW_TILE, seq)
    tm_out = tm_mlp = min(WIDE_ROW_TILE, n)
    tf_mlp = FF_TILE
    fox_tiles = (min(FOX_TQ, seq), min(FOX_TK, seq), min(FOX_TD, seq), FOX_HEADS_PER_STEP)
    tables = _ret_tables(seq)

    w_main, w_forget = _wprep(w_in)

    h = x.reshape(n, d)
    g_final = norm_final_g.reshape(1, d)
    for layer in range(depth):
        sgu_bias = jnp.repeat(jnp.transpose(sgu_b[layer]), LANES, axis=1)
        qkv, flog_t, y_a, y_b, y_d = _inmix(
            h, norm_mix_g[layer].reshape(1, d), w_main, w_forget, layer,
            pool_w[layer], pool_scale[layer].reshape(1, GROUP_WIDTH),
            sgu_norm_g[layer].reshape(1, GROUP_WIDTH), sgu_w_s[layer], sgu_bias,
            tables, ret_norm_g[layer].reshape(1, GROUP_WIDTH), seq, tm_in)
        bf_tab = jnp.zeros((SUBLANES, LANES), F32).at[:N_HEADS].set(jnp.broadcast_to(fox_b_f[layer][:, None], (N_HEADS, LANES)))
        y_c, (w_out_b, w_ff1_b, w_ff2_b) = _fox(qkv, flog_t, bf_tab, batch, seq, *fox_tiles,
                                                cast_weights=(w_out, w_ff1, w_ff2), layer=layer)

        h = _outproj(h, (y_a, y_b, y_c, y_d), w_out_b, tm_out)
        h = _mlp(h, norm_mlp_g[layer].reshape(1, d), w_ff1_b, w_ff2_b, g_final, tm_mlp, tf_mlp,
                 final_norm=(layer == depth - 1))
    return h.reshape(batch, seq, d)
```
